```python
import math
import jax, jax.numpy as jnp
from jax import lax
import numpy as np

D_MODEL = 1024
BATCH = 16
SEQ = 4096
DEPTH = 2

GRID_W = 64
PLE_DIM = 256
AB_WIDTH = D_MODEL // 2
NA_HEADS = 8
NA_HEAD_DIM = AB_WIDTH // NA_HEADS
NA_WIN_ROWS = 8
NA_WIN_COLS = 16
S5_GROUP = 16
S5_GROUPS = AB_WIDTH // S5_GROUP
S5_STATE = 64
RET_HEADS = 4
RET_QK_DIM = D_MODEL // RET_HEADS
RET_V_DIM = 2 * RET_QK_DIM
RET_CHUNK = 128
ROPE_BASE = 10000.0
D_FF = 2816
CONV_W = 3
N_EVEN = (DEPTH + 1) // 2
N_ODD = DEPTH // 2
EPS = 1e-6

kernel_name = 'hybrid_natten_s5_retnet_encoder'

F32 = jnp.float32


def rms_norm(x, g):
    xf = x.astype(F32)
    y = xf * lax.rsqrt(jnp.mean(xf * xf, axis=-1, keepdims=True) + EPS)
    return (y * g.astype(F32)).astype(x.dtype)


def head_rms(y):
    yf = y.astype(F32)
    return (yf * lax.rsqrt(jnp.mean(yf * yf, axis=-1, keepdims=True) + EPS)).astype(y.dtype)


def neighbourhood_attention(q, k, v, rpb):
    b, s, h, dh = q.shape
    rows = s // GRID_W
    kr = min(NA_WIN_ROWS, rows)
    kc = NA_WIN_COLS
    qg = q.reshape(b, rows, GRID_W, h, dh) * (dh ** -0.5)
    kg = k.reshape(b, rows, GRID_W, h, dh)
    vg = v.reshape(b, rows, GRID_W, h, dh)
    cols = jnp.arange(GRID_W)
    col_start = jnp.clip(cols - kc // 2, 0, GRID_W - kc)
    col_idx = col_start[:, None] + jnp.arange(kc)[None, :]
    dc_idx = col_idx - cols[:, None] + (NA_WIN_COLS - 1)

    def row_fn(r):
        rs = jnp.clip(r - kr // 2, 0, rows - kr)
        q_r = lax.dynamic_index_in_dim(qg, r, axis=1, keepdims=False)
        k_win = lax.dynamic_slice_in_dim(kg, rs, kr, axis=1)[:, :, col_idx]
        v_win = lax.dynamic_slice_in_dim(vg, rs, kr, axis=1)[:, :, col_idx]
        dr_idx = rs + jnp.arange(kr) - r + (NA_WIN_ROWS - 1)
        bias = rpb[:, dr_idx[:, None, None], dc_idx[None]].transpose(0, 2, 1, 3)
        logits = jnp.einsum('bchd,bicjhd->bhcij', q_r, k_win).astype(F32) + bias.astype(F32)
        probs = jax.nn.softmax(logits.reshape(b, h, GRID_W, kr * kc), axis=-1)
        probs = probs.reshape(logits.shape).astype(v.dtype)
        return jnp.einsum('bhcij,bicjhd->bchd', probs, v_win)

    out = lax.map(row_fn, jnp.arange(rows))
    return out.transpose(1, 0, 2, 3, 4).reshape(b, s, h * dh)


def _linear_recurrence(e1, e2):
    a1, b1 = e1
    a2, b2 = e2
    return a1 * a2, a2 * b1 + b2


def s5_bidirectional(u, lam_re, lam_im, log_dt, b_re, b_im, c_re, c_im, d_skip):
    bsz, s, _ = u.shape
    ug = u.astype(F32).reshape(bsz, s, S5_GROUPS, S5_GROUP)
    y = ug * d_skip.astype(F32).reshape(S5_GROUPS, S5_GROUP)
    for direction in range(2):
        lam = lax.complex(lam_re[direction].astype(F32), lam_im[direction].astype(F32))
        dt = jnp.exp(log_dt[direction].astype(F32))[:, None]
        lam_bar = jnp.exp(lam * dt)
        b_c = lax.complex(b_re[direction].astype(F32), b_im[direction].astype(F32))
        b_bar = ((lam_bar - 1.0) / lam)[:, :, None] * b_c
        bu = lax.complex(jnp.einsum('gph,bsgh->bsgp', jnp.real(b_bar), ug),
                         jnp.einsum('gph,bsgh->bsgp', jnp.imag(b_bar), ug))
        a = jnp.broadcast_to(lam_bar, (1, s, S5_GROUPS, S5_STATE))
        _, states = lax.associative_scan(_linear_recurrence, (a, bu), reverse=(direction == 1), axis=1)
        c = lax.complex(c_re[direction].astype(F32), c_im[direction].astype(F32))
        y = y + jnp.real(jnp.einsum('gnp,bsgp->bsgn', c, states))
    return y.reshape(bsz, s, AB_WIDTH).astype(u.dtype)


def s5_glu(y, w_glu, b_glu):
    yg = jax.nn.gelu(y, approximate=False)
    return yg * jax.nn.sigmoid(yg @ w_glu + b_glu)


def na_s5_mixer(h, w_in, rpb, lam_re, lam_im, log_dt, b_re, b_im, c_re, c_im, d_skip, w_glu, b_glu, w_out):
    b, s, _ = h.shape
    z = h @ w_in
    q, k, v, u = jnp.split(z, 4, axis=-1)
    shp = (b, s, NA_HEADS, NA_HEAD_DIM)
    a_out = neighbourhood_attention(q.reshape(shp), k.reshape(shp), v.reshape(shp), rpb)
    b_out = s5_glu(s5_bidirectional(u, lam_re, lam_im, log_dt, b_re, b_im, c_re, c_im, d_skip), w_glu, b_glu)
    return jnp.concatenate([a_out, b_out], axis=-1) @ w_out


def rotary(x):
    s, d = x.shape[1], x.shape[-1]
    half = d // 2
    inv_freq = ROPE_BASE ** (-jnp.arange(half, dtype=F32) / half)
    ang = jnp.arange(s, dtype=F32)[:, None] * inv_freq[None, :]
    cos = jnp.cos(ang)[None, :, None, :]
    sin = jnp.sin(ang)[None, :, None, :]
    xf = x.astype(F32)
    x1, x2 = xf[..., :half], xf[..., half:]
    return jnp.concatenate([x1 * cos - x2 * sin, x1 * sin + x2 * cos], axis=-1).astype(x.dtype)


def retention_bidirectional(q, k, v, log_gamma):
    b, s, h, dk = q.shape
    dv = v.shape[-1]
    L = RET_CHUNK
    n = s // L

    def chunks(t):
        return t.transpose(0, 2, 1, 3).reshape(b, h, n, L, t.shape[-1])

    qc, kc, vc = chunks(q), chunks(k), chunks(v)
    lg_f = log_gamma[0].astype(F32)[:, None]
    lg_b = log_gamma[1].astype(F32)[:, None]
    pos = jnp.arange(L, dtype=F32)
    diff = pos[:, None] - pos[None, :]
    decay = jnp.where(diff >= 0, jnp.exp(lg_f[:, :, None] * jnp.abs(diff)),
                      jnp.exp(lg_b[:, :, None] * jnp.abs(diff)))
    scores = jnp.einsum('bhnid,bhnjd->bhnij', qc, kc) * decay[:, None].astype(qc.dtype)
    out = jnp.einsum('bhnij,bhnje->bhnie', scores, vc)

    def to_scan(t):
        return t.transpose(2, 0, 1, 3, 4)

    vs = to_scan(vc)
    state0 = jnp.zeros((b, h, dk, dv), dtype=q.dtype)

    def cross(q_dec, k_dec, c_dec, reverse):
        cdec = c_dec[None, :, :, None].astype(q.dtype)

        def step(state, inp):
            qn, kn, vn = inp
            y = jnp.einsum('bhld,bhde->bhle', qn, state)
            state = cdec * state + jnp.einsum('bhld,bhle->bhde', kn, vn)
            return state, y

        qs = to_scan(qc * q_dec[None, :, None, :, None].astype(q.dtype))
        ks = to_scan(kc * k_dec[None, :, None, :, None].astype(q.dtype))
        _, ys = lax.scan(step, state0, (qs, ks, vs), reverse=reverse)
        return ys.transpose(1, 2, 0, 3, 4)

    fwd = cross(jnp.exp(lg_f * (pos + 1.0)), jnp.exp(lg_f * (L - 1.0 - pos)), jnp.exp(lg_f * L), False)
    bwd = cross(jnp.exp(lg_b * (L - pos)), jnp.exp(lg_b * pos), jnp.exp(lg_b * L), True)
    out = out + fwd + bwd
    return out.reshape(b, h, s, dv).transpose(0, 2, 1, 3)


def retention_mixer(h, w_in, decay_param, w_out):
    b, s, _ = h.shape
    z = h @ w_in
    q, k, v, g = jnp.split(z, [D_MODEL, 2 * D_MODEL, 2 * D_MODEL + RET_HEADS * RET_V_DIM], axis=-1)
    q = rotary(q.reshape(b, s, RET_HEADS, RET_QK_DIM))
    k = rotary(k.reshape(b, s, RET_HEADS, RET_QK_DIM)) * (RET_QK_DIM ** -0.5)
    v = v.reshape(b, s, RET_HEADS, RET_V_DIM)
    log_gamma = -jnp.exp(decay_param.astype(F32))
    y = head_rms(retention_bidirectional(q, k, v, log_gamma))
    y = jax.nn.silu(g) * y.reshape(b, s, RET_HEADS * RET_V_DIM)
    return y @ w_out


def conv_ffn(h, w_up, conv_w, conv_b, w_down):
    u = h @ w_up
    u = lax.conv_general_dilated(
        u, conv_w[:, None, :].astype(u.dtype), window_strides=(1,),
        padding=((CONV_W // 2, CONV_W // 2),), dimension_numbers=('NWC', 'WIO', 'NWC'),
        feature_group_count=u.shape[-1]) + conv_b
    a, g = jnp.split(u, 2, axis=-1)
    return (jax.nn.gelu(g, approximate=False) * a) @ w_down


def setup_inputs(seed: int = 0) -> dict:
    key = jax.random.key(seed)
    ks = jax.random.split(key, 32)

    def nrm(k, shape, scale):
        return scale * jax.random.normal(k, shape, F32)

    d = D_MODEL
    lam_im_base = jnp.pi * jnp.arange(S5_STATE, dtype=F32)
    ret_base = jnp.log(-jnp.log(1.0 - 2.0 ** (-5.0 - jnp.arange(RET_HEADS, dtype=F32))))
    ret_w_in_cols = 2 * d + 2 * RET_HEADS * RET_V_DIM
    return {
        'x': nrm(ks[0], (BATCH, SEQ, d), 1.0),
        'p': nrm(ks[1], (DEPTH, BATCH, SEQ, PLE_DIM), 1.0),
        'ab_norm': 1.0 + nrm(ks[2], (N_EVEN, d), 0.02),
        'ab_w_in': nrm(ks[3], (N_EVEN, d, 4 * AB_WIDTH), d ** -0.5),
        'na_rpb': nrm(ks[4], (N_EVEN, NA_HEADS, 2 * NA_WIN_ROWS - 1, 2 * NA_WIN_COLS - 1), 0.02),
        's5_lambda_re': -0.5 + nrm(ks[5], (N_EVEN, 2, S5_GROUPS, S5_STATE), 0.01),
        's5_lambda_im': lam_im_base + nrm(ks[6], (N_EVEN, 2, S5_GROUPS, S5_STATE), 0.01),
        's5_log_dt': jax.random.uniform(ks[7], (N_EVEN, 2, S5_GROUPS), F32, math.log(0.001), math.log(0.1)),
        's5_b_re': nrm(ks[8], (N_EVEN, 2, S5_GROUPS, S5_STATE, S5_GROUP), (2 * S5_GROUP) ** -0.5),
        's5_b_im': nrm(ks[9], (N_EVEN, 2, S5_GROUPS, S5_STATE, S5_GROUP), (2 * S5_GROUP) ** -0.5),
        's5_c_re': nrm(ks[10], (N_EVEN, 2, S5_GROUPS, S5_GROUP, S5_STATE), (2 * S5_STATE) ** -0.5),
        's5_c_im': nrm(ks[11], (N_EVEN, 2, S5_GROUPS, S5_GROUP, S5_STATE), (2 * S5_STATE) ** -0.5),
        's5_d': nrm(ks[12], (N_EVEN, AB_WIDTH), 1.0),
        's5_w_glu': nrm(ks[13], (N_EVEN, AB_WIDTH, AB_WIDTH), AB_WIDTH ** -0.5),
        's5_b_glu': nrm(ks[14], (N_EVEN, AB_WIDTH), 0.02),
        'ab_w_out': nrm(ks[15], (N_EVEN, d, d), d ** -0.5),
        'ret_norm': 1.0 + nrm(ks[16], (N_ODD, d), 0.02),
        'ret_w_in': nrm(ks[17], (N_ODD, d, ret_w_in_cols), d ** -0.5),
        'ret_decay': ret_base + nrm(ks[18], (N_ODD, 2, RET_HEADS), 0.05),
        'ret_w_out': nrm(ks[19], (N_ODD, RET_HEADS * RET_V_DIM, d), (RET_HEADS * RET_V_DIM) ** -0.5),
        'ffn_norm': 1.0 + nrm(ks[20], (DEPTH, d), 0.02),
        'ffn_w_up': nrm(ks[21], (DEPTH, d, 2 * D_FF), d ** -0.5),
        'ffn_conv_w': nrm(ks[22], (DEPTH, CONV_W, 2 * D_FF), CONV_W ** -0.5),
        'ffn_conv_b': nrm(ks[23], (DEPTH, 2 * D_FF), 0.02),
        'ffn_w_down': nrm(ks[24], (DEPTH, D_FF, d), D_FF ** -0.5),
        'ple_norm': 1.0 + nrm(ks[25], (DEPTH, d), 0.02),
        'ple_w_gate': nrm(ks[26], (DEPTH, d, d), d ** -0.5),
        'ple_w_proj': nrm(ks[27], (DEPTH, PLE_DIM, d), PLE_DIM ** -0.5),
        'final_norm': 1.0 + nrm(ks[28], (d,), 0.02),
    }


def reference(x, p, ab_norm, ab_w_in, na_rpb, s5_lambda_re, s5_lambda_im, s5_log_dt, s5_b_re, s5_b_im,
              s5_c_re, s5_c_im, s5_d, s5_w_glu, s5_b_glu, ab_w_out, ret_norm, ret_w_in, ret_decay, ret_w_out,
              ffn_norm, ffn_w_up, ffn_conv_w, ffn_conv_b, ffn_w_down, ple_norm, ple_w_gate, ple_w_proj,
              final_norm):
    for i in range(DEPTH):
        j = i // 2
        if i % 2 == 0:
            h = rms_norm(x, ab_norm[j])
            x = x + na_s5_mixer(h, ab_w_in[j], na_rpb[j], s5_lambda_re[j], s5_lambda_im[j], s5_log_dt[j],
                                s5_b_re[j], s5_b_im[j], s5_c_re[j], s5_c_im[j], s5_d[j], s5_w_glu[j],
                                s5_b_glu[j], ab_w_out[j])
        else:
            h = rms_norm(x, ret_norm[j])
            x = x + retention_mixer(h, ret_w_in[j], ret_decay[j], ret_w_out[j])
        x = x + conv_ffn(rms_norm(x, ffn_norm[i]), ffn_w_up[i], ffn_conv_w[i], ffn_conv_b[i], ffn_w_down[i])
        gate = jax.nn.sigmoid(rms_norm(x, ple_norm[i]) @ ple_w_gate[i])
        x = x + gate * (p[i] @ ple_w_proj[i])
    return rms_norm(x, final_norm)
```

```python
import functools
import math

import jax
import jax.numpy as jnp
from jax import lax
from jax.experimental import pallas as pl
from jax.experimental.pallas import tpu as pltpu

F32 = jnp.float32
BF16 = jnp.bfloat16

EPS = 1e-6
GRID_W = 64
NA_HEADS = 8
NA_HEAD_DIM = 64
NA_WIN_ROWS = 8
NA_WIN_COLS = 16
NA_MASK = -1e30
S5_GROUP = 16
S5_STATE = 64
S5_CHUNK = 16
RET_HEADS = 4
RET_QK = 256
RET_V = 512
RET_CHUNK = 256
ROPE_BASE = 10000.0
CONV_HALO = 16
VMEM_LIMIT = 56 * 1024 * 1024


def _cparams(sem):
    return pltpu.CompilerParams(dimension_semantics=sem, vmem_limit_bytes=VMEM_LIMIT)


def _const_spec(shape):
    nd = len(shape)
    return pl.BlockSpec(shape, lambda *_: (0,) * nd)


def _rms(xf, g):
    ms = jnp.mean(xf * xf, axis=-1, keepdims=True)
    return xf * lax.rsqrt(ms + EPS) * g


def _gelu(x):
    return 0.5 * x * (1.0 + lax.erf(x * (2.0 ** -0.5)))


def _sigmoid(x):
    return 1.0 / (1.0 + jnp.exp(-x))


def _norm_matmul_body(x_ref, g_ref, w_ref, o_ref, *, tn):
    h = _rms(x_ref[...], g_ref[...]).astype(BF16)
    n = w_ref.shape[1]
    for j in range(n // tn):
        o_ref[:, j * tn:(j + 1) * tn] = jnp.dot(
            h, w_ref[:, j * tn:(j + 1) * tn], preferred_element_type=F32).astype(o_ref.dtype)


def _norm_matmul(x, g, w, tm=512, tn=512):
    m, d = x.shape
    n = w.shape[1]
    return pl.pallas_call(
        functools.partial(_norm_matmul_body, tn=tn),
        grid=(m // tm,),
        in_specs=[pl.BlockSpec((tm, d), lambda i: (i, 0)), _const_spec((1, d)), _const_spec((d, n))],
        out_specs=pl.BlockSpec((tm, n), lambda i: (i, 0)),
        out_shape=jax.ShapeDtypeStruct((m, n), BF16),
        compiler_params=_cparams(("parallel",)),
        name="ab_in_proj",
    )(x, g.reshape(1, d), w)


def _na_bias_table(rpb):
    cols = jnp.arange(GRID_W)
    cs = jnp.clip(cols - NA_WIN_COLS // 2, 0, GRID_W - NA_WIN_COLS)
    j = jnp.arange(GRID_W)
    inwin = (j[None, :] >= cs[:, None]) & (j[None, :] < cs[:, None] + NA_WIN_COLS)
    dc = jnp.clip(j[None, :] - cols[:, None] + NA_WIN_COLS - 1, 0, 2 * NA_WIN_COLS - 2)
    t = rpb.astype(F32)[:, :, dc]
    t = jnp.where(inwin[None, None], t, NA_MASK).transpose(0, 2, 1, 3)
    h = rpb.shape[0]
    return jnp.stack([t[:, :, v:v + NA_WIN_ROWS, :].reshape(h, GRID_W, NA_WIN_ROWS * GRID_W)
                      for v in range(NA_WIN_ROWS)])


def _na_body(q_ref, k_ref, v_ref, bias_ref, o_ref, *, rows):
    kwin = NA_WIN_ROWS * GRID_W
    lane = lax.broadcasted_iota(jnp.int32, (GRID_W, 2 * NA_HEAD_DIM), 1)
    first = lane < NA_HEAD_DIM

    def row_fn(r, carry):
        rs = jnp.clip(r - NA_WIN_ROWS // 2, 0, rows - NA_WIN_ROWS)
        variant = rs - r + (NA_WIN_ROWS - 1)
        q_r = q_ref[0, pl.ds(pl.multiple_of(r * GRID_W, GRID_W), GRID_W), :]
        k_w = k_ref[0, pl.ds(pl.multiple_of(rs * GRID_W, GRID_W), kwin), :]
        v_w = v_ref[0, pl.ds(pl.multiple_of(rs * GRID_W, GRID_W), kwin), :]
        for pair in range(NA_HEADS // 2):
            sl = slice(pair * 2 * NA_HEAD_DIM, (pair + 1) * 2 * NA_HEAD_DIM)
            q_p, k_p, v_p = q_r[:, sl], k_w[:, sl], v_w[:, sl]
            outs = []
            for sub in range(2):
                head = 2 * pair + sub
                keep = first if sub == 0 else jnp.logical_not(first)
                q_m = jnp.where(keep, q_p, jnp.zeros_like(q_p))
                logits = lax.dot_general(q_m, k_p, (((1,), (1,)), ((), ())),
                                         preferred_element_type=F32)
                logits = logits + bias_ref[variant, head]
                mx = jnp.max(logits, axis=-1, keepdims=True)
                e = jnp.exp(logits - mx)
                den = jnp.sum(e, axis=-1, keepdims=True)
                o = jnp.dot(e.astype(BF16), v_p, preferred_element_type=F32)
                outs.append(o / den)
            o_pair = jnp.where(first, outs[0], outs[1])
            o_ref[0, pl.ds(pl.multiple_of(r * GRID_W, GRID_W), GRID_W), sl] = o_pair.astype(o_ref.dtype)
        return carry

    lax.fori_loop(0, rows, row_fn, 0)


def _na_attention(z, bias):
    b, s, _ = z.shape
    width = NA_HEADS * NA_HEAD_DIM
    rows = s // GRID_W
    spec = lambda c: pl.BlockSpec((1, s, width), lambda i, c=c: (i, 0, c))
    return pl.pallas_call(
        functools.partial(_na_body, rows=rows),
        grid=(b,),
        in_specs=[spec(0), spec(1), spec(2), _const_spec(bias.shape)],
        out_specs=pl.BlockSpec((1, s, width), lambda i: (i, 0, 0)),
        out_shape=jax.ShapeDtypeStruct((b, s, width), BF16),
        compiler_params=_cparams(("parallel",)),
        name="na_attention",
    )(z, z, z, bias)


def _s5_matrices(lam_re, lam_im, log_dt, b_re, b_im, c_re, c_im, d_skip):
    L, hg, p = S5_CHUNK, S5_GROUP, S5_STATE
    g = lam_re.shape[1]
    tau = jnp.arange(L + 1, dtype=F32)
    pw, bbar, cc = [], [], []
    for d in range(2):
        lam = lax.complex(lam_re[d].astype(F32), lam_im[d].astype(F32))
        lam_dt = lam * jnp.exp(log_dt[d].astype(F32))[:, None]
        lam_bar = jnp.exp(lam_dt)
        pw.append(jnp.exp(lam_dt[None] * tau[:, None, None]))
        b_c = lax.complex(b_re[d].astype(F32), b_im[d].astype(F32))
        bbar.append(((lam_bar - 1.0) / lam)[:, :, None] * b_c)
        cc.append(lax.complex(c_re[d].astype(F32), c_im[d].astype(F32)))

    kf = jnp.real(jnp.einsum('gnp,tgp,gph->tgnh', cc[0], pw[0][:L], bbar[0]))
    kb = jnp.real(jnp.einsum('gnp,tgp,gph->tgnh', cc[1], pw[1][:L], bbar[1]))
    s_i = jnp.arange(L)[:, None]
    t_i = jnp.arange(L)[None, :]
    lag_f = jnp.clip(t_i - s_i, 0, L - 1)
    lag_b = jnp.clip(s_i - t_i, 0, L - 1)
    m = (jnp.where((t_i >= s_i)[:, :, None, None, None], kf[lag_f], 0.0)
         + jnp.where((s_i >= t_i)[:, :, None, None, None], kb[lag_b], 0.0))
    skip = (jnp.eye(L, dtype=F32)[:, :, None, None, None]
            * jnp.eye(hg, dtype=F32)[None, None, None]
            * d_skip.astype(F32).reshape(g, hg)[None, None, :, :, None])
    m_intra = (m + skip).transpose(2, 0, 4, 1, 3).reshape(g, L * hg, L * hg)

    a_f = pw[0][L - 1 - jnp.arange(L)][:, :, :, None] * bbar[0][None]
    a_b = pw[1][jnp.arange(L)][:, :, :, None] * bbar[1][None]
    loc = jnp.stack([jnp.real(a_f), jnp.imag(a_f), jnp.real(a_b), jnp.imag(a_b)])
    m_loc = loc.transpose(2, 1, 4, 0, 3).reshape(g, L * hg, 4, p)

    e_f = cc[0][None] * pw[0][1 + jnp.arange(L)][:, :, None, :]
    e_b = cc[1][None] * pw[1][L - jnp.arange(L)][:, :, None, :]
    cr = jnp.stack([jnp.real(e_f), -jnp.imag(e_f), jnp.real(e_b), -jnp.imag(e_b)])
    m_cross = cr.transpose(2, 0, 4, 1, 3).reshape(g, 4, p, L * hg)

    gp = g // 2
    eye2 = jnp.eye(2, dtype=F32)
    ml = m_loc.reshape(gp, 2, L * hg, 4, p)
    ml = jnp.einsum('qiukp,ij->qiukjp', ml, eye2).reshape(gp, 2 * L * hg, 4 * 2 * p)
    mc = m_cross.reshape(gp, 2, 4, p, L * hg)
    mc = jnp.einsum('qikpu,ij->qkipju', mc, eye2).reshape(gp, 4 * 2 * p, 2 * L * hg)
    mi = m_intra.reshape(gp, 2, L * hg, L * hg)
    mi = jnp.einsum('qiuv,ij->qiujv', mi, eye2).reshape(gp, 2 * L * hg, 2 * L * hg)
    lam_l = jnp.stack([jnp.real(pw[0][L]), jnp.imag(pw[0][L]),
                       jnp.real(pw[1][L]), jnp.imag(pw[1][L])])
    lam_l = lam_l.reshape(4, gp, 2 * p).transpose(1, 0, 2)
    return ml.astype(BF16), mi.astype(BF16), mc.astype(BF16), lam_l


def _s5_body(u_ref, mloc_ref, mintra_ref, mcross_ref, lam_ref, y_ref, xloc_ref, xin_ref, *, bsz, tr):
    rows = u_ref.shape[1]
    nc = rows // bsz
    w = lam_ref.shape[2]

    def loc_fn(i, carry):
        rs = pl.ds(pl.multiple_of(i * tr, tr), tr)
        xloc_ref[rs, :] = jnp.dot(u_ref[0, rs, :], mloc_ref[0], preferred_element_type=F32)
        return carry

    lax.fori_loop(0, rows // tr, loc_fn, 0)

    afr, afi, abr, abi = lam_ref[0, 0:1, :], lam_ref[0, 1:2, :], lam_ref[0, 2:3, :], lam_ref[0, 3:4, :]

    def carry_fn(i, state):
        sfr, sfi, sbr, sbi = state
        rf = pl.ds(pl.multiple_of(i * bsz, bsz), bsz)
        rb = pl.ds(pl.multiple_of((nc - 1 - i) * bsz, bsz), bsz)
        xin_ref[rf, 0 * w:1 * w] = sfr
        xin_ref[rf, 1 * w:2 * w] = sfi
        xin_ref[rb, 2 * w:3 * w] = sbr
        xin_ref[rb, 3 * w:4 * w] = sbi
        nfr = afr * sfr - afi * sfi + xloc_ref[rf, 0 * w:1 * w]
        nfi = afr * sfi + afi * sfr + xloc_ref[rf, 1 * w:2 * w]
        nbr = abr * sbr - abi * sbi + xloc_ref[rb, 2 * w:3 * w]
        nbi = abr * sbi + abi * sbr + xloc_ref[rb, 3 * w:4 * w]
        return nfr, nfi, nbr, nbi

    zero = jnp.zeros((bsz, w), F32)
    lax.fori_loop(0, nc, carry_fn, (zero, zero, zero, zero))

    def out_fn(i, carry):
        rs = pl.ds(pl.multiple_of(i * tr, tr), tr)
        y = jnp.dot(u_ref[0, rs, :], mintra_ref[0], preferred_element_type=F32)
        y = y + jnp.dot(xin_ref[rs, :].astype(BF16), mcross_ref[0], preferred_element_type=F32)
        y_ref[0, rs, :] = y
        return carry

    lax.fori_loop(0, rows // tr, out_fn, 0)


def _s5_scan(u_pairs, mats, bsz):
    mloc, mintra, mcross, lam_l = mats
    gp, rows, width = u_pairs.shape
    tr = min(512, rows)
    pair_spec = lambda a: pl.BlockSpec((1,) + a.shape[1:], lambda i: (i, 0, 0))
    return pl.pallas_call(
        functools.partial(_s5_body, bsz=bsz, tr=tr),
        grid=(gp,),
        in_specs=[pair_spec(u_pairs), pair_spec(mloc), pair_spec(mintra), pair_spec(mcross),
                  pair_spec(lam_l)],
        out_specs=pl.BlockSpec((1, rows, width), lambda i: (i, 0, 0)),
        out_shape=jax.ShapeDtypeStruct((gp, rows, width), F32),
        scratch_shapes=[pltpu.VMEM((rows, mloc.shape[2]), F32), pltpu.VMEM((rows, mloc.shape[2]), F32)],
        compiler_params=_cparams(("parallel",)),
        name="s5_scan",
    )(u_pairs, mloc, mintra, mcross, lam_l)


def _ab_out_body(x_ref, a_ref, y_ref, wglu_ref, bglu_ref, wout_ref, o_ref):
    half = a_ref.shape[1]
    yg = _gelu(y_ref[...])
    gate = _sigmoid(jnp.dot(yg.astype(BF16), wglu_ref[...], preferred_element_type=F32) + bglu_ref[...])
    b_out = (yg * gate).astype(BF16)
    acc = x_ref[...] + jnp.dot(a_ref[...], wout_ref[0:half, :], preferred_element_type=F32)
    o_ref[...] = acc + jnp.dot(b_out, wout_ref[half:, :], preferred_element_type=F32)


def _ab_out(x, a, y, wglu, bglu, wout, tm=512):
    m, d = x.shape
    half = a.shape[1]
    row = lambda n: pl.BlockSpec((tm, n), lambda i: (i, 0))
    return pl.pallas_call(
        _ab_out_body,
        grid=(m // tm,),
        in_specs=[row(d), row(half), row(half), _const_spec(wglu.shape), _const_spec((1, half)),
                  _const_spec(wout.shape)],
        out_specs=row(d),
        out_shape=jax.ShapeDtypeStruct((m, d), F32),
        compiler_params=_cparams(("parallel",)),
        name="ab_out_proj",
    )(x, a, y, wglu, bglu.reshape(1, half), wout)


def _ffn_body(x_ref, xp_ref, xn_ref, g_ref, wup_ref, cw_ref, cb_ref, wdn_ref, o_ref, h_ref, *,
              tiles_per_seq, tf):
    tm = x_ref.shape[0]
    dff = wdn_ref.shape[0]
    ext = tm + 2 * CONV_HALO
    i = pl.program_id(0)
    has_prev = (i % tiles_per_seq != 0).astype(F32)
    has_next = (i % tiles_per_seq != tiles_per_seq - 1).astype(F32)
    g = g_ref[...]
    x = x_ref[...]
    h_ref[0:tm, :] = _rms(x, g).astype(BF16)
    h_ref[tm:tm + CONV_HALO, :] = (_rms(xn_ref[...], g) * has_next).astype(BF16)
    h_ref[tm + CONV_HALO:ext, :] = (_rms(xp_ref[...], g) * has_prev).astype(BF16)
    h = h_ref[...]

    def conv(u, c0):
        w = cw_ref[:, c0:c0 + tf]
        out = (pltpu.roll(u, 1, 0) * w[0:1] + u * w[1:2] + pltpu.roll(u, ext - 1, 0) * w[2:3]
               + cb_ref[:, c0:c0 + tf])
        return out[0:tm]

    acc = x
    for j in range(dff // tf):
        ua = jnp.dot(h, wup_ref[:, j * tf:(j + 1) * tf], preferred_element_type=F32)
        ug = jnp.dot(h, wup_ref[:, dff + j * tf:dff + (j + 1) * tf], preferred_element_type=F32)
        act = (_gelu(conv(ug, dff + j * tf)) * conv(ua, j * tf)).astype(BF16)
        acc = acc + jnp.dot(act, wdn_ref[j * tf:(j + 1) * tf, :], preferred_element_type=F32)
    o_ref[...] = acc


def _conv_ffn(x, seq, g, wup, cw, cb, wdn, tm=512, tf=256):
    m, d = x.shape
    dff = wdn.shape[0]
    hb = tm // CONV_HALO
    nblk = m // CONV_HALO
    return pl.pallas_call(
        functools.partial(_ffn_body, tiles_per_seq=seq // tm, tf=tf),
        grid=(m // tm,),
        in_specs=[pl.BlockSpec((tm, d), lambda i: (i, 0)),
                  pl.BlockSpec((CONV_HALO, d), lambda i: (jnp.maximum(i * hb - 1, 0), 0)),
                  pl.BlockSpec((CONV_HALO, d), lambda i: (jnp.minimum((i + 1) * hb, nblk - 1), 0)),
                  _const_spec((1, d)), _const_spec(wup.shape), _const_spec(cw.shape),
                  _const_spec((1, 2 * dff)), _const_spec(wdn.shape)],
        out_specs=pl.BlockSpec((tm, d), lambda i: (i, 0)),
        out_shape=jax.ShapeDtypeStruct((m, d), F32),
        scratch_shapes=[pltpu.VMEM((tm + 2 * CONV_HALO, d), BF16)],
        compiler_params=_cparams(("parallel",)),
        name="conv_ffn",
    )(x, x, x, g.reshape(1, d), wup, cw, cb.reshape(1, 2 * dff), wdn)


def _ple_body(x_ref, p_ref, g_ref, wg_ref, wp_ref, fg_ref, o_ref, *, final):
    x = x_ref[...]
    gate = _sigmoid(jnp.dot(_rms(x, g_ref[...]).astype(BF16), wg_ref[...], preferred_element_type=F32))
    out = x + gate * jnp.dot(p_ref[...].astype(BF16), wp_ref[...], preferred_element_type=F32)
    if final:
        out = _rms(out, fg_ref[...])
    o_ref[...] = out


def _ple(x, p, g, wg, wp, fg, final, tm=512):
    m, d = x.shape
    pd = p.shape[1]
    return pl.pallas_call(
        functools.partial(_ple_body, final=final),
        grid=(m // tm,),
        in_specs=[pl.BlockSpec((tm, d), lambda i: (i, 0)), pl.BlockSpec((tm, pd), lambda i: (i, 0)),
                  _const_spec((1, d)), _const_spec(wg.shape), _const_spec(wp.shape), _const_spec((1, d))],
        out_specs=pl.BlockSpec((tm, d), lambda i: (i, 0)),
        out_shape=jax.ShapeDtypeStruct((m, d), F32),
        compiler_params=_cparams(("parallel",)),
        name="ple_gate",
    )(x, p, g.reshape(1, d), wg, wp, fg.reshape(1, d))


def _ret_in_body(x_ref, g_ref, w_ref, cos_ref, sin_ref, q_ref, k_ref, v_ref, gt_ref):
    h = _rms(x_ref[...], g_ref[...]).astype(BF16)
    cos, sin = cos_ref[...], sin_ref[...]
    half = RET_QK // 2
    dq = q_ref.shape[1]
    dv = v_ref.shape[1]

    def rot(z, scale):
        x1, x2 = z[:, :half], z[:, half:]
        return jnp.concatenate([(x1 * cos - x2 * sin) * scale, (x1 * sin + x2 * cos) * scale], axis=-1)

    for j in range(dq // RET_QK):
        c0 = j * RET_QK
        zq = jnp.dot(h, w_ref[:, c0:c0 + RET_QK], preferred_element_type=F32)
        q_ref[:, c0:c0 + RET_QK] = rot(zq, 1.0).astype(BF16)
        zk = jnp.dot(h, w_ref[:, dq + c0:dq + c0 + RET_QK], preferred_element_type=F32)
        k_ref[:, c0:c0 + RET_QK] = rot(zk, RET_QK ** -0.5).astype(BF16)
    tn = 512
    for j in range(dv // tn):
        c0 = j * tn
        v_ref[:, c0:c0 + tn] = jnp.dot(h, w_ref[:, 2 * dq + c0:2 * dq + c0 + tn],
                                       preferred_element_type=F32).astype(BF16)
        gt_ref[:, c0:c0 + tn] = jnp.dot(h, w_ref[:, 2 * dq + dv + c0:2 * dq + dv + c0 + tn],
                                        preferred_element_type=F32).astype(BF16)


def _ret_in_proj(x, seq, g, w, cos, sin, tm=512):
    m, d = x.shape
    dq = RET_HEADS * RET_QK
    dv = RET_HEADS * RET_V
    tps = seq // tm
    row = lambda n: pl.BlockSpec((tm, n), lambda i: (i, 0))
    pos = pl.BlockSpec((tm, RET_QK // 2), lambda i: (i % tps, 0))
    return pl.pallas_call(
        _ret_in_body,
        grid=(m // tm,),
        in_specs=[row(d), _const_spec((1, d)), _const_spec(w.shape), pos, pos],
        out_specs=[row(dq), row(dq), row(dv), row(dv)],
        out_shape=[jax.ShapeDtypeStruct((m, dq), BF16), jax.ShapeDtypeStruct((m, dq), BF16),
                   jax.ShapeDtypeStruct((m, dv), BF16), jax.ShapeDtypeStruct((m, dv), BF16)],
        compiler_params=_cparams(("parallel",)),
        name="ret_in_proj",
    )(x, g.reshape(1, d), w, cos, sin)


def _ret_tables(decay_param):
    L = RET_CHUNK
    lg = -jnp.exp(decay_param.astype(F32))
    lg_f, lg_b = lg[0][:, None], lg[1][:, None]
    pos = jnp.arange(L, dtype=F32)
    diff = pos[:, None] - pos[None, :]
    dmat = jnp.where(diff >= 0, jnp.exp(lg_f[:, :, None] * jnp.abs(diff)),
                     jnp.exp(lg_b[:, :, None] * jnp.abs(diff)))
    vecs = jnp.stack([jnp.exp(lg_f * (pos + 1.0)),
                      jnp.exp(lg_f * (L - 1.0 - pos)),
                      jnp.exp(lg_b * (L - pos)),
                      jnp.exp(lg_b * pos)], axis=-1)
    cdec = jnp.exp(lg * L).T
    return dmat, vecs, cdec


def _ret_body(cdec_ref, q_ref, k_ref, v_ref, gt_ref, dmat_ref, vec_ref, o_ref, acc_ref, st_ref):
    L = RET_CHUNK
    seq = q_ref.shape[1]
    n = seq // L
    hd = pl.program_id(1)
    cf = cdec_ref[hd, 0]
    cb = cdec_ref[hd, 1]
    vec = vec_ref[0]
    qf, kf, qb, kb = vec[:, 0:1], vec[:, 1:2], vec[:, 2:3], vec[:, 3:4]
    contract0 = (((0,), (0,)), ((), ()))

    st_ref[...] = jnp.zeros_like(st_ref)

    def bwd(i, carry):
        c = n - 1 - i
        rs = pl.ds(pl.multiple_of(c * L, L), L)
        q_c, k_c, v_c = q_ref[0, rs, :], k_ref[0, rs, :], v_ref[0, rs, :]
        acc_ref[rs, :] = qb * jnp.dot(q_c, st_ref[...].astype(BF16), preferred_element_type=F32)
        k_s = (k_c.astype(F32) * kb).astype(BF16)
        st_ref[...] = cb * st_ref[...] + lax.dot_general(k_s, v_c, contract0, preferred_element_type=F32)
        return carry

    lax.fori_loop(0, n, bwd, 0)

    st_ref[...] = jnp.zeros_like(st_ref)

    def fwd(c, carry):
        rs = pl.ds(pl.multiple_of(c * L, L), L)
        q_c, k_c, v_c = q_ref[0, rs, :], k_ref[0, rs, :], v_ref[0, rs, :]
        scores = lax.dot_general(q_c, k_c, (((1,), (1,)), ((), ())), preferred_element_type=F32)
        y = jnp.dot((scores * dmat_ref[0]).astype(BF16), v_c, preferred_element_type=F32)
        y = y + qf * jnp.dot(q_c, st_ref[...].astype(BF16), preferred_element_type=F32)
        y = y + acc_ref[rs, :]
        k_s = (k_c.astype(F32) * kf).astype(BF16)
        st_ref[...] = cf * st_ref[...] + lax.dot_general(k_s, v_c, contract0, preferred_element_type=F32)
        y = y * lax.rsqrt(jnp.mean(y * y, axis=-1, keepdims=True) + EPS)
        gt = gt_ref[0, rs, :].astype(F32)
        o_ref[0, rs, :] = (gt * _sigmoid(gt) * y).astype(o_ref.dtype)
        return carry

    lax.fori_loop(0, n, fwd, 0)


def _retention(q, k, v, gt, tables):
    dmat, vecs, cdec = tables
    b, s, _ = q.shape
    L = RET_CHUNK
    qk_spec = pl.BlockSpec((1, s, RET_QK), lambda i, h, *_: (i, 0, h))
    v_spec = pl.BlockSpec((1, s, RET_V), lambda i, h, *_: (i, 0, h))
    grid_spec = pltpu.PrefetchScalarGridSpec(
        num_scalar_prefetch=1,
        grid=(b, RET_HEADS),
        in_specs=[qk_spec, qk_spec, v_spec, v_spec,
                  pl.BlockSpec((1, L, L), lambda i, h, *_: (h, 0, 0)),
                  pl.BlockSpec((1, L, 4), lambda i, h, *_: (h, 0, 0))],
        out_specs=v_spec,
        scratch_shapes=[pltpu.VMEM((s, RET_V), F32), pltpu.VMEM((RET_QK, RET_V), F32)],
    )
    return pl.pallas_call(
        _ret_body,
        grid_spec=grid_spec,
        out_shape=jax.ShapeDtypeStruct((b, s, RET_HEADS * RET_V), BF16),
        compiler_params=_cparams(("parallel", "parallel")),
        name="retention",
    )(cdec, q, k, v, gt, dmat, vecs)


def _proj_residual_body(x_ref, y_ref, w_ref, o_ref):
    o_ref[...] = x_ref[...] + jnp.dot(y_ref[...], w_ref[...], preferred_element_type=F32)


def _proj_residual(x, y, w, tm=512):
    m, d = x.shape
    kd = y.shape[1]
    return pl.pallas_call(
        _proj_residual_body,
        grid=(m // tm,),
        in_specs=[pl.BlockSpec((tm, d), lambda i: (i, 0)), pl.BlockSpec((tm, kd), lambda i: (i, 0)),
                  _const_spec(w.shape)],
        out_specs=pl.BlockSpec((tm, d), lambda i: (i, 0)),
        out_shape=jax.ShapeDtypeStruct((m, d), F32),
        compiler_params=_cparams(("parallel",)),
        name="ret_out_proj",
    )(x, y, w)


def _na_s5_layer(x, bsz, seq, norm, w_in, rpb, lam_re, lam_im, log_dt, b_re, b_im, c_re, c_im, d_skip,
                 w_glu, b_glu, w_out):
    m, d = x.shape
    half = NA_HEADS * NA_HEAD_DIM
    groups = half // S5_GROUP
    col_scale = jnp.concatenate([jnp.full((half,), NA_HEAD_DIM ** -0.5, F32), jnp.ones((3 * half,), F32)])
    z = _norm_matmul(x, norm, (w_in * col_scale).astype(BF16)).reshape(bsz, seq, 4 * half)
    a_out = _na_attention(z, _na_bias_table(rpb)).reshape(m, half)

    nc = seq // S5_CHUNK
    u = z[:, :, 3 * half:].reshape(bsz, nc, S5_CHUNK, groups // 2, 2, S5_GROUP)
    u = u.transpose(3, 1, 0, 4, 2, 5).reshape(groups // 2, nc * bsz, 2 * S5_CHUNK * S5_GROUP)
    mats = _s5_matrices(lam_re, lam_im, log_dt, b_re, b_im, c_re, c_im, d_skip)
    y = _s5_scan(u, mats, bsz).reshape(groups // 2, nc, bsz, 2, S5_CHUNK, S5_GROUP)
    y = y.transpose(2, 1, 4, 0, 3, 5).reshape(m, half)
    return _ab_out(x, a_out, y, w_glu.astype(BF16), b_glu, w_out.astype(BF16))


def _rope_tables(seq):
    half = RET_QK // 2
    inv_freq = ROPE_BASE ** (-jnp.arange(half, dtype=F32) / half)
    ang = jnp.arange(seq, dtype=F32)[:, None] * inv_freq[None, :]
    return jnp.cos(ang), jnp.sin(ang)


def _retention_layer(x, bsz, seq, norm, w_in, decay, w_out):
    m, d = x.shape
    cos, sin = _rope_tables(seq)
    q, k, v, gt = _ret_in_proj(x, seq, norm, w_in.astype(BF16), cos, sin)
    shp = lambda t: t.reshape(bsz, seq, t.shape[-1])
    y = _retention(shp(q), shp(k), shp(v), shp(gt), _ret_tables(decay))
    return _proj_residual(x, y.reshape(m, -1), w_out.astype(BF16))


def kernel(x, p, ab_norm, ab_w_in, na_rpb, s5_lambda_re, s5_lambda_im, s5_log_dt, s5_b_re, s5_b_im,
           s5_c_re, s5_c_im, s5_d, s5_w_glu, s5_b_glu, ab_w_out, ret_norm, ret_w_in, ret_decay, ret_w_out,
           ffn_norm, ffn_w_up, ffn_conv_w, ffn_conv_b, ffn_w_down, ple_norm, ple_w_gate, ple_w_proj,
           final_norm):
    bsz, seq, d = x.shape
    depth = p.shape[0]
    m = bsz * seq
    xs = x.reshape(m, d)
    for i in range(depth):
        j = i // 2
        if i % 2 == 0:
            xs = _na_s5_layer(xs, bsz, seq, ab_norm[j], ab_w_in[j], na_rpb[j], s5_lambda_re[j],
                              s5_lambda_im[j], s5_log_dt[j], s5_b_re[j], s5_b_im[j], s5_c_re[j],
                              s5_c_im[j], s5_d[j], s5_w_glu[j], s5_b_glu[j], ab_w_out[j])
        else:
            xs = _retention_layer(xs, bsz, seq, ret_norm[j], ret_w_in[j], ret_decay[j], ret_w_out[j])
        xs = _conv_ffn(xs, seq, ffn_norm[i], ffn_w_up[i].astype(BF16), ffn_conv_w[i], ffn_conv_b[i],
                       ffn_w_down[i].astype(BF16))
        xs = _ple(xs, p[i].reshape(m, -1), ple_norm[i], ple_w_gate[i].astype(BF16),
                  ple_w_proj[i].astype(BF16), final_norm, final=(i == depth - 1))
    return xs.reshape(bsz, seq, d)
```

```python
import functools

import jax
import jax.numpy as jnp
from jax import lax
from jax.experimental import pallas as pl
from jax.experimental.pallas import tpu as pltpu

F32 = jnp.float32
BF16 = jnp.bfloat16

LANES = 128
EPS = 1e-6
GRID_W = 64
NA_HEADS = 8
NA_HEAD_DIM = 64
NA_WIN_ROWS = 8
NA_WIN_COLS = 16
NA_MASK = -1e30
NA_HEAD_BLOCK = 4
S5_GROUP = 16
S5_STATE = 64
S5_CHUNK = 16
S5_SLOT = 2 * S5_GROUP
S5_SLOTS = LANES // S5_SLOT
AB_TOKENS = 32
RET_HEADS = 4
RET_QK = 256
RET_V = 512
RET_CHUNK = 256
ROPE_BASE = 10000.0
CONV_HALO = 16
VMEM_LIMIT = 56 * 1024 * 1024


def _cparams(sem):
    return pltpu.CompilerParams(dimension_semantics=sem, vmem_limit_bytes=VMEM_LIMIT)


def _const_spec(shape):
    nd = len(shape)
    return pl.BlockSpec(shape, lambda *_: (0,) * nd, pipeline_mode=pl.Buffered(1))


def _rms(xf, g):
    ms = jnp.mean(xf * xf, axis=-1, keepdims=True)
    return xf * lax.rsqrt(ms + EPS) * g


def _gelu(x):
    return 0.5 * x * (1.0 + lax.erf(x * (2.0 ** -0.5)))


def _sigmoid(x):
    return 1.0 / (1.0 + jnp.exp(-x))


def _slot_transpose4(a, lane):
    lo = lane < 2 * S5_SLOT
    even = (lane % (2 * S5_SLOT)) < S5_SLOT
    b0 = jnp.where(lo, a[0], pltpu.roll(a[2], 2 * S5_SLOT, 1))
    b2 = jnp.where(lo, pltpu.roll(a[0], 2 * S5_SLOT, 1), a[2])
    b1 = jnp.where(lo, a[1], pltpu.roll(a[3], 2 * S5_SLOT, 1))
    b3 = jnp.where(lo, pltpu.roll(a[1], 2 * S5_SLOT, 1), a[3])
    return [jnp.where(even, b0, pltpu.roll(b1, S5_SLOT, 1)),
            jnp.where(even, pltpu.roll(b0, LANES - S5_SLOT, 1), b1),
            jnp.where(even, b2, pltpu.roll(b3, S5_SLOT, 1)),
            jnp.where(even, pltpu.roll(b2, LANES - S5_SLOT, 1), b3)]


def _ab_in_body(x_ref, g_ref, w_ref, z_ref, u_ref, zs_ref, *, tn):
    bsz, tt, d = x_ref.shape
    h = _rms(x_ref[...].reshape(bsz * tt, d), g_ref[...]).astype(BF16)
    nz = z_ref.shape[2]
    for j in range(nz // tn):
        z_ref[:, :, j * tn:(j + 1) * tn] = jnp.dot(
            h, w_ref[:, j * tn:(j + 1) * tn], preferred_element_type=F32
        ).astype(z_ref.dtype).reshape(bsz, tt, tn)
    zu = jnp.dot(h, w_ref[:, nz:], preferred_element_type=F32)
    for w in range(zs_ref.shape[0]):
        zs_ref[w] = zu[:, w * LANES:(w + 1) * LANES]
    lane = lax.broadcasted_iota(jnp.int32, (bsz, LANES), 1)
    for cl in range(tt // S5_CHUNK):
        for v in range(S5_CHUNK // S5_SLOTS):
            for w in range(zs_ref.shape[0]):
                pos = cl * S5_CHUNK + S5_SLOTS * v
                outs = _slot_transpose4([zs_ref[w, pl.ds(pos + i, bsz, stride=tt), :]
                                         for i in range(S5_SLOTS)], lane)
                for j in range(S5_SLOTS):
                    u_ref[S5_SLOTS * w + j, cl, :, v * LANES:(v + 1) * LANES] = outs[j].astype(u_ref.dtype)


def _ab_in_proj(x, g, w, nz, tn=512):
    bsz, seq, d = x.shape
    nu = w.shape[1] - nz
    pairs = nu // S5_SLOT
    assert S5_CHUNK * S5_SLOT == nu and pairs == (nu // LANES) * S5_SLOTS
    tt = AB_TOKENS
    return pl.pallas_call(
        functools.partial(_ab_in_body, tn=tn),
        grid=(seq // tt,),
        in_specs=[pl.BlockSpec((bsz, tt, d), lambda i: (0, i, 0)), _const_spec((1, d)), _const_spec(w.shape)],
        out_specs=[pl.BlockSpec((bsz, tt, nz), lambda i: (0, i, 0)),
                   pl.BlockSpec((pairs, tt // S5_CHUNK, bsz, nu), lambda i: (0, i, 0, 0))],
        out_shape=[jax.ShapeDtypeStruct((bsz, seq, nz), BF16),
                   jax.ShapeDtypeStruct((pairs, seq // S5_CHUNK, bsz, nu), BF16)],
        scratch_shapes=[pltpu.VMEM((nu // LANES, bsz * tt, LANES), F32)],
        compiler_params=_cparams(("parallel",)),
        name="ab_in_proj",
    )(x, g.reshape(1, d), w)


def _na_bias_table(rpb):
    cols = jnp.arange(GRID_W)
    cs = jnp.clip(cols - NA_WIN_COLS // 2, 0, GRID_W - NA_WIN_COLS)
    j = jnp.arange(GRID_W)
    inwin = (j[None, :] >= cs[:, None]) & (j[None, :] < cs[:, None] + NA_WIN_COLS)
    dc = jnp.clip(j[None, :] - cols[:, None] + NA_WIN_COLS - 1, 0, 2 * NA_WIN_COLS - 2)
    t = rpb.astype(F32)[:, :, dc]
    t = jnp.where(inwin[None, None], t, NA_MASK).transpose(0, 2, 1, 3)
    h = rpb.shape[0]
    kw = NA_WIN_ROWS * GRID_W
    return jnp.stack([t[:, :, v:v + NA_WIN_ROWS, :].reshape(h // NA_HEAD_BLOCK, NA_HEAD_BLOCK * GRID_W, kw)
                      for v in range(NA_WIN_ROWS)])


def _na_body(q_ref, k_ref, v_ref, bias_ref, o_ref, *, rows):
    kwin = NA_WIN_ROWS * GRID_W
    bw = NA_HEAD_BLOCK * NA_HEAD_DIM
    head_of_lane = lax.broadcasted_iota(jnp.int32, (GRID_W, bw), 1) // NA_HEAD_DIM

    def row_fn(r, carry):
        rs = jnp.clip(r - NA_WIN_ROWS // 2, 0, rows - NA_WIN_ROWS)
        variant = rs - r + (NA_WIN_ROWS - 1)
        qrow = pl.ds(pl.multiple_of(r * GRID_W, GRID_W), GRID_W)
        krow = pl.ds(pl.multiple_of(rs * GRID_W, GRID_W), kwin)
        for blk in range(NA_HEADS // NA_HEAD_BLOCK):
            sl = slice(blk * bw, (blk + 1) * bw)
            q_p, k_p, v_p = q_ref[0, qrow, sl], k_ref[0, krow, sl], v_ref[0, krow, sl]
            q_all = jnp.concatenate([jnp.where(head_of_lane == hh, q_p, jnp.zeros_like(q_p))
                                     for hh in range(NA_HEAD_BLOCK)], axis=0)
            logits = lax.dot_general(q_all, k_p, (((1,), (1,)), ((), ())), preferred_element_type=F32)
            logits = logits + bias_ref[variant, blk]
            e = jnp.exp(logits - jnp.max(logits, axis=-1, keepdims=True))
            den = jnp.sum(e, axis=-1, keepdims=True)
            o = jnp.dot(e.astype(BF16), v_p, preferred_element_type=F32) / den
            out = o[0:GRID_W]
            for hh in range(1, NA_HEAD_BLOCK):
                out = jnp.where(head_of_lane == hh, o[hh * GRID_W:(hh + 1) * GRID_W], out)
            o_ref[0, qrow, sl] = out.astype(o_ref.dtype)
        return carry

    lax.fori_loop(0, rows, row_fn, 0, unroll=2)


def _na_attention(z, bias):
    b, s, _ = z.shape
    width = NA_HEADS * NA_HEAD_DIM
    rows = s // GRID_W
    spec = lambda c: pl.BlockSpec((1, s, width), lambda i, c=c: (i, 0, c))
    return pl.pallas_call(
        functools.partial(_na_body, rows=rows),
        grid=(b,),
        in_specs=[spec(0), spec(1), spec(2), _const_spec(bias.shape)],
        out_specs=pl.BlockSpec((1, s, width), lambda i: (i, 0, 0)),
        out_shape=jax.ShapeDtypeStruct((b, s, width), BF16),
        compiler_params=_cparams(("parallel",)),
        name="na_attention",
    )(z, z, z, bias)


def _s5_matrices(lam_re, lam_im, log_dt, b_re, b_im, c_re, c_im, d_skip):
    L, hg, p = S5_CHUNK, S5_GROUP, S5_STATE
    g = lam_re.shape[1]
    tau = jnp.arange(L + 1, dtype=F32)
    pw, bbar, cc = [], [], []
    for d in range(2):
        lam = lax.complex(lam_re[d].astype(F32), lam_im[d].astype(F32))
        lam_dt = lam * jnp.exp(log_dt[d].astype(F32))[:, None]
        lam_bar = jnp.exp(lam_dt)
        pw.append(jnp.exp(lam_dt[None] * tau[:, None, None]))
        b_c = lax.complex(b_re[d].astype(F32), b_im[d].astype(F32))
        bbar.append(((lam_bar - 1.0) / lam)[:, :, None] * b_c)
        cc.append(lax.complex(c_re[d].astype(F32), c_im[d].astype(F32)))

    kf = jnp.real(jnp.einsum('gnp,tgp,gph->tgnh', cc[0], pw[0][:L], bbar[0]))
    kb = jnp.real(jnp.einsum('gnp,tgp,gph->tgnh', cc[1], pw[1][:L], bbar[1]))
    s_i = jnp.arange(L)[:, None]
    t_i = jnp.arange(L)[None, :]
    lag_f = jnp.clip(t_i - s_i, 0, L - 1)
    lag_b = jnp.clip(s_i - t_i, 0, L - 1)
    m = (jnp.where((t_i >= s_i)[:, :, None, None, None], kf[lag_f], 0.0)
         + jnp.where((s_i >= t_i)[:, :, None, None, None], kb[lag_b], 0.0))
    skip = (jnp.eye(L, dtype=F32)[:, :, None, None, None]
            * jnp.eye(hg, dtype=F32)[None, None, None]
            * d_skip.astype(F32).reshape(g, hg)[None, None, :, :, None])
    m_intra = (m + skip).transpose(2, 0, 4, 1, 3)

    a_f = pw[0][L - 1 - jnp.arange(L)][:, :, :, None] * bbar[0][None]
    a_b = pw[1][jnp.arange(L)][:, :, :, None] * bbar[1][None]
    loc = jnp.stack([jnp.real(a_f), jnp.imag(a_f), jnp.real(a_b), jnp.imag(a_b)])
    m_loc = loc.transpose(2, 1, 4, 0, 3)

    e_f = cc[0][None] * pw[0][1 + jnp.arange(L)][:, :, None, :]
    e_b = cc[1][None] * pw[1][L - jnp.arange(L)][:, :, None, :]
    cr = jnp.stack([jnp.real(e_f), -jnp.imag(e_f), jnp.real(e_b), -jnp.imag(e_b)])
    m_cross = cr.transpose(2, 0, 4, 1, 3)

    gp = g // 2
    eye2 = jnp.eye(2, dtype=F32)
    ml = jnp.einsum('qishkp,ij->qsihkjp', m_loc.reshape(gp, 2, L, hg, 4, p), eye2)
    ml = ml.reshape(gp, L * 2 * hg, 4 * 2 * p)
    mc = jnp.einsum('qikpln,ij->qkjplin', m_cross.reshape(gp, 2, 4, p, L, hg), eye2)
    mc = mc.reshape(gp, 4 * 2 * p, L * 2 * hg)
    mi = jnp.einsum('qishtn,ij->qsihtjn', m_intra.reshape(gp, 2, L, hg, L, hg), eye2)
    mi = mi.reshape(gp, L * 2 * hg, L * 2 * hg)
    lam_l = jnp.stack([jnp.real(pw[0][L]), jnp.imag(pw[0][L]),
                       jnp.real(pw[1][L]), jnp.imag(pw[1][L])])
    lam_l = lam_l.reshape(4, gp, 2 * p).transpose(1, 0, 2)
    return ml.astype(BF16), mi.astype(BF16), mc.astype(BF16), lam_l


def _s5_body(u_ref, mloc_ref, mintra_ref, mcross_ref, lam_ref, y_ref, xloc_ref, xin_ref, *, bsz, tr):
    rows = u_ref.shape[1]
    nc = rows // bsz
    w = lam_ref.shape[2]

    def loc_fn(i, carry):
        rs = pl.ds(pl.multiple_of(i * tr, tr), tr)
        xloc = jnp.dot(u_ref[0, rs, :], mloc_ref[0], preferred_element_type=F32)
        for part in range(4):
            xloc_ref[part, rs, :] = xloc[:, part * w:(part + 1) * w]
        return carry

    lax.fori_loop(0, rows // tr, loc_fn, 0)

    afr, afi, abr, abi = lam_ref[0, 0:1, :], lam_ref[0, 1:2, :], lam_ref[0, 2:3, :], lam_ref[0, 3:4, :]

    def carry_fn(i, state):
        sfr, sfi, sbr, sbi = state
        rf = pl.ds(pl.multiple_of(i * bsz, bsz), bsz)
        rb = pl.ds(pl.multiple_of((nc - 1 - i) * bsz, bsz), bsz)
        xin_ref[0, rf, :] = sfr
        xin_ref[1, rf, :] = sfi
        xin_ref[2, rb, :] = sbr
        xin_ref[3, rb, :] = sbi
        nfr = afr * sfr - afi * sfi + xloc_ref[0, rf, :]
        nfi = afr * sfi + afi * sfr + xloc_ref[1, rf, :]
        nbr = abr * sbr - abi * sbi + xloc_ref[2, rb, :]
        nbi = abr * sbi + abi * sbr + xloc_ref[3, rb, :]
        return nfr, nfi, nbr, nbi

    zero = jnp.zeros((bsz, w), F32)
    lax.fori_loop(0, nc, carry_fn, (zero, zero, zero, zero))

    def out_fn(i, carry):
        rs = pl.ds(pl.multiple_of(i * tr, tr), tr)
        y = jnp.dot(u_ref[0, rs, :], mintra_ref[0], preferred_element_type=F32)
        xin = jnp.concatenate([xin_ref[part, rs, :] for part in range(4)], axis=1)
        y = y + jnp.dot(xin.astype(BF16), mcross_ref[0], preferred_element_type=F32)
        y_ref[0, rs, :] = y
        return carry

    lax.fori_loop(0, rows // tr, out_fn, 0)


def _s5_scan(u_pairs, mats, bsz):
    mloc, mintra, mcross, lam_l = mats
    gp, rows, width = u_pairs.shape
    tr = min(512, rows)
    pair_spec = lambda a: pl.BlockSpec((1,) + a.shape[1:], lambda i: (i, 0, 0))
    return pl.pallas_call(
        functools.partial(_s5_body, bsz=bsz, tr=tr),
        grid=(gp,),
        in_specs=[pair_spec(u_pairs), pair_spec(mloc), pair_spec(mintra), pair_spec(mcross),
                  pair_spec(lam_l)],
        out_specs=pl.BlockSpec((1, rows, width), lambda i: (i, 0, 0)),
        out_shape=jax.ShapeDtypeStruct((gp, rows, width), F32),
        scratch_shapes=[pltpu.VMEM((4, rows, lam_l.shape[2]), F32), pltpu.VMEM((4, rows, lam_l.shape[2]), F32)],
        compiler_params=_cparams(("parallel",)),
        name="s5_scan",
    )(u_pairs, mloc, mintra, mcross, lam_l)


def _ab_out_body(x_ref, a_ref, y_ref, wglu_ref, bglu_ref, wout_ref, o_ref, ys_ref):
    bsz, tt, d = x_ref.shape
    half = a_ref.shape[2]
    lane = lax.broadcasted_iota(jnp.int32, (bsz, LANES), 1)
    for cl in range(tt // S5_CHUNK):
        for v in range(S5_CHUNK // S5_SLOTS):
            for w in range(half // LANES):
                outs = _slot_transpose4([y_ref[S5_SLOTS * w + j, cl, :, v * LANES:(v + 1) * LANES]
                                         for j in range(S5_SLOTS)], lane)
                pos = cl * S5_CHUNK + S5_SLOTS * v
                for i in range(S5_SLOTS):
                    ys_ref[w, pl.ds(pos + i, bsz, stride=tt), :] = outs[i]
    yg = _gelu(jnp.concatenate([ys_ref[w] for w in range(half // LANES)], axis=1))
    gate = _sigmoid(jnp.dot(yg.astype(BF16), wglu_ref[...], preferred_element_type=F32) + bglu_ref[...])
    b_out = (yg * gate).astype(BF16)
    a = a_ref[...].reshape(bsz * tt, half)
    acc = x_ref[...].reshape(bsz * tt, d) + jnp.dot(a, wout_ref[0:half, :], preferred_element_type=F32)
    acc = acc + jnp.dot(b_out, wout_ref[half:, :], preferred_element_type=F32)
    o_ref[...] = acc.reshape(bsz, tt, d)


def _ab_out(x, a, y_pairs, wglu, bglu, wout):
    bsz, seq, d = x.shape
    half = a.shape[2]
    pairs = y_pairs.shape[0]
    tt = AB_TOKENS
    tok = lambda n: pl.BlockSpec((bsz, tt, n), lambda i: (0, i, 0))
    return pl.pallas_call(
        _ab_out_body,
        grid=(seq // tt,),
        in_specs=[tok(d), tok(half),
                  pl.BlockSpec((pairs, tt // S5_CHUNK, bsz, half), lambda i: (0, i, 0, 0)),
                  _const_spec(wglu.shape), _const_spec((1, half)), _const_spec(wout.shape)],
        out_specs=tok(d),
        out_shape=jax.ShapeDtypeStruct((bsz, seq, d), F32),
        scratch_shapes=[pltpu.VMEM((half // LANES, bsz * tt, LANES), F32)],
        compiler_params=_cparams(("parallel",)),
        name="ab_out_proj",
    )(x, a, y_pairs, wglu, bglu.reshape(1, half), wout)


def _ffn_body(x_ref, xp_ref, xn_ref, g_ref, wup_ref, cw_ref, cb_ref, wdn_ref, o_ref, h_ref, act_ref, *,
              tiles_per_seq, tf):
    tm = x_ref.shape[0]
    dff = wdn_ref.shape[0]
    ext = tm + 2 * CONV_HALO
    i = pl.program_id(0)
    has_prev = (i % tiles_per_seq != 0).astype(F32)
    has_next = (i % tiles_per_seq != tiles_per_seq - 1).astype(F32)
    g = g_ref[...]
    h_ref[0:tm, :] = _rms(x_ref[...], g).astype(BF16)
    h_ref[tm:tm + CONV_HALO, :] = (_rms(xn_ref[...], g) * has_next).astype(BF16)
    h_ref[tm + CONV_HALO:ext, :] = (_rms(xp_ref[...], g) * has_prev).astype(BF16)
    h = h_ref[...]

    def conv(u, c0):
        w = cw_ref[:, c0:c0 + tf]
        out = (pltpu.roll(u, 1, 0) * w[0:1] + u * w[1:2] + pltpu.roll(u, ext - 1, 0) * w[2:3]
               + cb_ref[:, c0:c0 + tf])
        return out[0:tm]

    for j in range(dff // tf):
        ua = jnp.dot(h, wup_ref[:, j * tf:(j + 1) * tf], preferred_element_type=F32)
        ug = jnp.dot(h, wup_ref[:, dff + j * tf:dff + (j + 1) * tf], preferred_element_type=F32)
        act_ref[:, j * tf:(j + 1) * tf] = (_gelu(conv(ug, dff + j * tf)) * conv(ua, j * tf)).astype(BF16)
    o_ref[...] = x_ref[...] + jnp.dot(act_ref[...], wdn_ref[...], preferred_element_type=F32)


def _conv_ffn(x, seq, g, wup, cw, cb, wdn, tm=512, tf=256):
    m, d = x.shape
    dff = wdn.shape[0]
    hb = tm // CONV_HALO
    nblk = m // CONV_HALO
    return pl.pallas_call(
        functools.partial(_ffn_body, tiles_per_seq=seq // tm, tf=tf),
        grid=(m // tm,),
        in_specs=[pl.BlockSpec((tm, d), lambda i: (i, 0)),
                  pl.BlockSpec((CONV_HALO, d), lambda i: (jnp.maximum(i * hb - 1, 0), 0)),
                  pl.BlockSpec((CONV_HALO, d), lambda i: (jnp.minimum((i + 1) * hb, nblk - 1), 0)),
                  _const_spec((1, d)), _const_spec(wup.shape), _const_spec(cw.shape),
                  _const_spec((1, 2 * dff)), _const_spec(wdn.shape)],
        out_specs=pl.BlockSpec((tm, d), lambda i: (i, 0)),
        out_shape=jax.ShapeDtypeStruct((m, d), F32),
        scratch_shapes=[pltpu.VMEM((tm + 2 * CONV_HALO, d), BF16), pltpu.VMEM((tm, dff), BF16)],
        compiler_params=_cparams(("parallel",)),
        name="conv_ffn",
    )(x, x, x, g.reshape(1, d), wup, cw, cb.reshape(1, 2 * dff), wdn)


def _ple_body(x_ref, p_ref, g_ref, wg_ref, wp_ref, fg_ref, o_ref, *, final):
    x = x_ref[...]
    gate = _sigmoid(jnp.dot(_rms(x, g_ref[...]).astype(BF16), wg_ref[...], preferred_element_type=F32))
    out = x + gate * jnp.dot(p_ref[...].astype(BF16), wp_ref[...], preferred_element_type=F32)
    if final:
        out = _rms(out, fg_ref[...])
    o_ref[...] = out


def _ple(x, p, g, wg, wp, fg, final, tm=512):
    m, d = x.shape
    pd = p.shape[1]
    return pl.pallas_call(
        functools.partial(_ple_body, final=final),
        grid=(m // tm,),
        in_specs=[pl.BlockSpec((tm, d), lambda i: (i, 0)), pl.BlockSpec((tm, pd), lambda i: (i, 0)),
                  _const_spec((1, d)), _const_spec(wg.shape), _const_spec(wp.shape), _const_spec((1, d))],
        out_specs=pl.BlockSpec((tm, d), lambda i: (i, 0)),
        out_shape=jax.ShapeDtypeStruct((m, d), F32),
        compiler_params=_cparams(("parallel",)),
        name="ple_gate",
    )(x, p, g.reshape(1, d), wg, wp, fg.reshape(1, d))


def _ret_in_body(x_ref, g_ref, w_ref, cos_ref, sin_ref, q_ref, k_ref, v_ref, gt_ref):
    h = _rms(x_ref[...], g_ref[...]).astype(BF16)
    cos, sin = cos_ref[...], sin_ref[...]
    half = RET_QK // 2
    dq = q_ref.shape[1]
    dv = v_ref.shape[1]

    def rot(z, scale):
        x1, x2 = z[:, :half], z[:, half:]
        return jnp.concatenate([(x1 * cos - x2 * sin) * scale, (x1 * sin + x2 * cos) * scale], axis=-1)

    for j in range(dq // RET_QK):
        c0 = j * RET_QK
        zq = jnp.dot(h, w_ref[:, c0:c0 + RET_QK], preferred_element_type=F32)
        q_ref[:, c0:c0 + RET_QK] = rot(zq, 1.0).astype(BF16)
        zk = jnp.dot(h, w_ref[:, dq + c0:dq + c0 + RET_QK], preferred_element_type=F32)
        k_ref[:, c0:c0 + RET_QK] = rot(zk, RET_QK ** -0.5).astype(BF16)
    tn = 512
    for j in range(dv // tn):
        c0 = j * tn
        v_ref[:, c0:c0 + tn] = jnp.dot(h, w_ref[:, 2 * dq + c0:2 * dq + c0 + tn],
                                       preferred_element_type=F32).astype(BF16)
        gt_ref[:, c0:c0 + tn] = jnp.dot(h, w_ref[:, 2 * dq + dv + c0:2 * dq + dv + c0 + tn],
                                        preferred_element_type=F32).astype(BF16)


def _ret_in_proj(x, seq, g, w, cos, sin, tm=512):
    m, d = x.shape
    dq = RET_HEADS * RET_QK
    dv = RET_HEADS * RET_V
    tps = seq // tm
    row = lambda n: pl.BlockSpec((tm, n), lambda i: (i, 0))
    pos = pl.BlockSpec((tm, RET_QK // 2), lambda i: (i % tps, 0))
    return pl.pallas_call(
        _ret_in_body,
        grid=(m // tm,),
        in_specs=[row(d), _const_spec((1, d)), _const_spec(w.shape), pos, pos],
        out_specs=[row(dq), row(dq), row(dv), row(dv)],
        out_shape=[jax.ShapeDtypeStruct((m, dq), BF16), jax.ShapeDtypeStruct((m, dq), BF16),
                   jax.ShapeDtypeStruct((m, dv), BF16), jax.ShapeDtypeStruct((m, dv), BF16)],
        compiler_params=_cparams(("parallel",)),
        name="ret_in_proj",
    )(x, g.reshape(1, d), w, cos, sin)


def _ret_tables(decay_param):
    L = RET_CHUNK
    lg = -jnp.exp(decay_param.astype(F32))
    lg_f, lg_b = lg[0][:, None], lg[1][:, None]
    pos = jnp.arange(L, dtype=F32)
    diff = pos[:, None] - pos[None, :]
    dmat = jnp.where(diff >= 0, jnp.exp(lg_f[:, :, None] * jnp.abs(diff)),
                     jnp.exp(lg_b[:, :, None] * jnp.abs(diff)))
    vecs = jnp.stack([jnp.exp(lg_f * (pos + 1.0)),
                      jnp.exp(lg_f * (L - 1.0 - pos)),
                      jnp.exp(lg_b * (L - pos)),
                      jnp.exp(lg_b * pos)], axis=-1)
    cdec = jnp.exp(lg * L).T
    return dmat, vecs, cdec


def _ret_body(cdec_ref, q_ref, k_ref, v_ref, gt_ref, dmat_ref, vec_ref, o_ref, acc_ref, st_ref):
    L = RET_CHUNK
    seq = q_ref.shape[1]
    n = seq // L
    hd = pl.program_id(1)
    cf = cdec_ref[hd, 0]
    cb = cdec_ref[hd, 1]
    vec = vec_ref[0]
    qf, kf, qb, kb = vec[:, 0:1], vec[:, 1:2], vec[:, 2:3], vec[:, 3:4]
    contract0 = (((0,), (0,)), ((), ()))

    st_ref[...] = jnp.zeros_like(st_ref)

    def bwd(i, carry):
        c = n - 1 - i
        rs = pl.ds(pl.multiple_of(c * L, L), L)
        q_c, k_c, v_c = q_ref[0, rs, :], k_ref[0, rs, :], v_ref[0, rs, :]
        acc_ref[rs, :] = qb * jnp.dot(q_c, st_ref[...].astype(BF16), preferred_element_type=F32)
        k_s = (k_c.astype(F32) * kb).astype(BF16)
        st_ref[...] = cb * st_ref[...] + lax.dot_general(k_s, v_c, contract0, preferred_element_type=F32)
        return carry

    lax.fori_loop(0, n, bwd, 0, unroll=2)

    st_ref[...] = jnp.zeros_like(st_ref)

    def fwd(c, carry):
        rs = pl.ds(pl.multiple_of(c * L, L), L)
        q_c, k_c, v_c = q_ref[0, rs, :], k_ref[0, rs, :], v_ref[0, rs, :]
        scores = lax.dot_general(q_c, k_c, (((1,), (1,)), ((), ())), preferred_element_type=F32)
        y = jnp.dot((scores * dmat_ref[0]).astype(BF16), v_c, preferred_element_type=F32)
        y = y + qf * jnp.dot(q_c, st_ref[...].astype(BF16), preferred_element_type=F32)
        y = y + acc_ref[rs, :]
        k_s = (k_c.astype(F32) * kf).astype(BF16)
        st_ref[...] = cf * st_ref[...] + lax.dot_general(k_s, v_c, contract0, preferred_element_type=F32)
        y = y * lax.rsqrt(jnp.mean(y * y, axis=-1, keepdims=True) + EPS)
        gt = gt_ref[0, rs, :].astype(F32)
        o_ref[0, rs, :] = (gt * _sigmoid(gt) * y).astype(o_ref.dtype)
        return carry

    lax.fori_loop(0, n, fwd, 0, unroll=2)


def _retention(q, k, v, gt, tables):
    dmat, vecs, cdec = tables
    b, s, _ = q.shape
    L = RET_CHUNK
    qk_spec = pl.BlockSpec((1, s, RET_QK), lambda i, h, *_: (i, 0, h))
    v_spec = pl.BlockSpec((1, s, RET_V), lambda i, h, *_: (i, 0, h))
    grid_spec = pltpu.PrefetchScalarGridSpec(
        num_scalar_prefetch=1,
        grid=(b, RET_HEADS),
        in_specs=[qk_spec, qk_spec, v_spec, v_spec,
                  pl.BlockSpec((1, L, L), lambda i, h, *_: (h, 0, 0)),
                  pl.BlockSpec((1, L, 4), lambda i, h, *_: (h, 0, 0))],
        out_specs=v_spec,
        scratch_shapes=[pltpu.VMEM((s, RET_V), F32), pltpu.VMEM((RET_QK, RET_V), F32)],
    )
    return pl.pallas_call(
        _ret_body,
        grid_spec=grid_spec,
        out_shape=jax.ShapeDtypeStruct((b, s, RET_HEADS * RET_V), BF16),
        compiler_params=_cparams(("parallel", "parallel")),
        name="retention",
    )(cdec, q, k, v, gt, dmat, vecs)


def _proj_residual_body(x_ref, y_ref, w_ref, o_ref):
    o_ref[...] = x_ref[...] + jnp.dot(y_ref[...], w_ref[...], preferred_element_type=F32)


def _proj_residual(x, y, w, tm=512):
    m, d = x.shape
    kd = y.shape[1]
    return pl.pallas_call(
        _proj_residual_body,
        grid=(m // tm,),
        in_specs=[pl.BlockSpec((tm, d), lambda i: (i, 0)), pl.BlockSpec((tm, kd), lambda i: (i, 0)),
                  _const_spec(w.shape)],
        out_specs=pl.BlockSpec((tm, d), lambda i: (i, 0)),
        out_shape=jax.ShapeDtypeStruct((m, d), F32),
        compiler_params=_cparams(("parallel",)),
        name="ret_out_proj",
    )(x, y, w)


def _na_s5_layer(x, bsz, seq, norm, w_in, rpb, lam_re, lam_im, log_dt, b_re, b_im, c_re, c_im, d_skip,
                 w_glu, b_glu, w_out):
    m, d = x.shape
    half = NA_HEADS * NA_HEAD_DIM
    col_scale = jnp.concatenate([jnp.full((half,), NA_HEAD_DIM ** -0.5, F32), jnp.ones((3 * half,), F32)])
    z, u_pairs = _ab_in_proj(x.reshape(bsz, seq, d), norm, (w_in * col_scale).astype(BF16), 3 * half)
    a_out = _na_attention(z, _na_bias_table(rpb))
    mats = _s5_matrices(lam_re, lam_im, log_dt, b_re, b_im, c_re, c_im, d_skip)
    pairs, nc = u_pairs.shape[0], u_pairs.shape[1]
    y_pairs = _s5_scan(u_pairs.reshape(pairs, nc * bsz, half), mats, bsz).reshape(pairs, nc, bsz, half)
    out = _ab_out(x.reshape(bsz, seq, d), a_out, y_pairs, w_glu.astype(BF16), b_glu, w_out.astype(BF16))
    return out.reshape(m, d)


def _rope_tables(seq):
    half = RET_QK // 2
    inv_freq = ROPE_BASE ** (-jnp.arange(half, dtype=F32) / half)
    ang = jnp.arange(seq, dtype=F32)[:, None] * inv_freq[None, :]
    return jnp.cos(ang), jnp.sin(ang)


def _retention_layer(x, bsz, seq, norm, w_in, decay, w_out):
    m, d = x.shape
    cos, sin = _rope_tables(seq)
    q, k, v, gt = _ret_in_proj(x, seq, norm, w_in.astype(BF16), cos, sin)
    shp = lambda t: t.reshape(bsz, seq, t.shape[-1])
    y = _retention(shp(q), shp(k), shp(v), shp(gt), _ret_tables(decay))
    return _proj_residual(x, y.reshape(m, -1), w_out.astype(BF16))


def kernel(x, p, ab_norm, ab_w_in, na_rpb, s5_lambda_re, s5_lambda_im, s5_log_dt, s5_b_re, s5_b_im,
           s5_c_re, s5_c_im, s5_d, s5_w_glu, s5_b_glu, ab_w_out, ret_norm, ret_w_in, ret_decay, ret_w_out,
           ffn_norm, ffn_w_up, ffn_conv_w, ffn_conv_b, ffn_w_down, ple_norm, ple_w_gate, ple_w_proj,
           final_norm):
    bsz, seq, d = x.shape
    depth = p.shape[0]
    m = bsz * seq
    xs = x.reshape(m, d)
    for i in range(depth):
        j = i // 2
        if i % 2 == 0:
            xs = _na_s5_layer(xs, bsz, seq, ab_norm[j], ab_w_in[j], na_rpb[j], s5_lambda_re[j],
                              s5_lambda_im[j], s5_log_dt[j], s5_b_re[j], s5_b_im[j], s5_c_re[j],
                              s5_c_im[j], s5_d[j], s5_w_glu[j], s5_b_glu[j], ab_w_out[j])
        else:
            xs = _retention_layer(xs, bsz, seq, ret_norm[j], ret_w_in[j], ret_decay[j], ret_w_out[j])
        xs = _conv_ffn(xs, seq, ffn_norm[i], ffn_w_up[i].astype(BF16), ffn_conv_w[i], ffn_conv_b[i],
                       ffn_w_down[i].astype(BF16))
        xs = _ple(xs, p[i].reshape(m, -1), ple_norm[i], ple_w_gate[i].astype(BF16),
                  ple_w_proj[i].astype(BF16), final_norm, final=(i == depth - 1))
    return xs.reshape(bsz, seq, d)
```

```python
import functools

import jax
import jax.numpy as jnp
from jax import lax
from jax.experimental import pallas as pl
from jax.experimental.pallas import tpu as pltpu

F32 = jnp.float32
BF16 = jnp.bfloat16

LANES = 128
EPS = 1e-6
GRID_W = 64
NA_HEADS = 8
NA_HEAD_DIM = 64
NA_WIN_ROWS = 8
NA_WIN_COLS = 16
NA_MASK = -1e30
NA_HEAD_BLOCK = 4
S5_GROUP = 16
S5_STATE = 64
S5_CHUNK = 16
S5_SLOT = 2 * S5_GROUP
S5_SLOTS = LANES // S5_SLOT
AB_TOKENS = 32
RET_HEADS = 4
RET_QK = 256
RET_V = 512
RET_CHUNK = 256
ROPE_BASE = 10000.0
CONV_HALO = 16
VMEM_LIMIT = 56 * 1024 * 1024


def _cparams(sem):
    return pltpu.CompilerParams(dimension_semantics=sem, vmem_limit_bytes=VMEM_LIMIT)


def _const_spec(shape):
    nd = len(shape)
    return pl.BlockSpec(shape, lambda *_: (0,) * nd, pipeline_mode=pl.Buffered(1))


def _rms(xf, g):
    ms = jnp.mean(xf * xf, axis=-1, keepdims=True)
    return xf * lax.rsqrt(ms + EPS) * g


def _gelu(x):
    return 0.5 * x * (1.0 + lax.erf(x * (2.0 ** -0.5)))


def _sigmoid(x):
    return 1.0 / (1.0 + jnp.exp(-x))


def _slot_transpose4(a, lane):
    lo = lane < 2 * S5_SLOT
    even = (lane % (2 * S5_SLOT)) < S5_SLOT
    b0 = jnp.where(lo, a[0], pltpu.roll(a[2], 2 * S5_SLOT, 1))
    b2 = jnp.where(lo, pltpu.roll(a[0], 2 * S5_SLOT, 1), a[2])
    b1 = jnp.where(lo, a[1], pltpu.roll(a[3], 2 * S5_SLOT, 1))
    b3 = jnp.where(lo, pltpu.roll(a[1], 2 * S5_SLOT, 1), a[3])
    return [jnp.where(even, b0, pltpu.roll(b1, S5_SLOT, 1)),
            jnp.where(even, pltpu.roll(b0, LANES - S5_SLOT, 1), b1),
            jnp.where(even, b2, pltpu.roll(b3, S5_SLOT, 1)),
            jnp.where(even, pltpu.roll(b2, LANES - S5_SLOT, 1), b3)]


def _ab_in_body(x_ref, g_ref, w_ref, z_ref, u_ref, zs_ref, *, tn):
    bsz, tt, d = x_ref.shape
    h = _rms(x_ref[...].reshape(bsz * tt, d), g_ref[...]).astype(BF16)
    nz = z_ref.shape[2]
    for j in range(nz // tn):
        z_ref[:, :, j * tn:(j + 1) * tn] = jnp.dot(
            h, w_ref[:, j * tn:(j + 1) * tn], preferred_element_type=F32
        ).astype(z_ref.dtype).reshape(bsz, tt, tn)
    zu = jnp.dot(h, w_ref[:, nz:], preferred_element_type=F32)
    for w in range(zs_ref.shape[0]):
        zs_ref[w] = zu[:, w * LANES:(w + 1) * LANES]
    lane = lax.broadcasted_iota(jnp.int32, (bsz, LANES), 1)
    for cl in range(tt // S5_CHUNK):
        for v in range(S5_CHUNK // S5_SLOTS):
            for w in range(zs_ref.shape[0]):
                pos = cl * S5_CHUNK + S5_SLOTS * v
                outs = _slot_transpose4([zs_ref[w, pl.ds(pos + i, bsz, stride=tt), :]
                                         for i in range(S5_SLOTS)], lane)
                for j in range(S5_SLOTS):
                    u_ref[S5_SLOTS * w + j, cl, :, v * LANES:(v + 1) * LANES] = outs[j].astype(u_ref.dtype)


def _ab_in_proj(x, g, w, nz, tn=512):
    bsz, seq, d = x.shape
    nu = w.shape[1] - nz
    pairs = nu // S5_SLOT
    assert S5_CHUNK * S5_SLOT == nu and pairs == (nu // LANES) * S5_SLOTS
    tt = AB_TOKENS
    return pl.pallas_call(
        functools.partial(_ab_in_body, tn=tn),
        grid=(seq // tt,),
        in_specs=[pl.BlockSpec((bsz, tt, d), lambda i: (0, i, 0)), _const_spec((1, d)), _const_spec(w.shape)],
        out_specs=[pl.BlockSpec((bsz, tt, nz), lambda i: (0, i, 0)),
                   pl.BlockSpec((pairs, tt // S5_CHUNK, bsz, nu), lambda i: (0, i, 0, 0))],
        out_shape=[jax.ShapeDtypeStruct((bsz, seq, nz), BF16),
                   jax.ShapeDtypeStruct((pairs, seq // S5_CHUNK, bsz, nu), BF16)],
        scratch_shapes=[pltpu.VMEM((nu // LANES, bsz * tt, LANES), F32)],
        compiler_params=_cparams(("parallel",)),
        name="ab_in_proj",
    )(x, g.reshape(1, d), w)


def _na_bias_table(rpb):
    cols = jnp.arange(GRID_W)
    cs = jnp.clip(cols - NA_WIN_COLS // 2, 0, GRID_W - NA_WIN_COLS)
    j = jnp.arange(GRID_W)
    inwin = (j[None, :] >= cs[:, None]) & (j[None, :] < cs[:, None] + NA_WIN_COLS)
    dc = jnp.clip(j[None, :] - cols[:, None] + NA_WIN_COLS - 1, 0, 2 * NA_WIN_COLS - 2)
    t = rpb.astype(F32)[:, :, dc]
    t = jnp.where(inwin[None, None], t, NA_MASK).transpose(0, 2, 1, 3)
    h = rpb.shape[0]
    kw = NA_WIN_ROWS * GRID_W
    return jnp.stack([t[:, :, v:v + NA_WIN_ROWS, :].reshape(h // NA_HEAD_BLOCK, NA_HEAD_BLOCK * GRID_W, kw)
                      for v in range(NA_WIN_ROWS)])


def _na_body(q_ref, k_ref, v_ref, bias_ref, o_ref, *, rows):
    kwin = NA_WIN_ROWS * GRID_W
    bw = NA_HEAD_BLOCK * NA_HEAD_DIM
    head_of_lane = lax.broadcasted_iota(jnp.int32, (GRID_W, bw), 1) // NA_HEAD_DIM

    def row_fn(r, carry):
        rs = jnp.clip(r - NA_WIN_ROWS // 2, 0, rows - NA_WIN_ROWS)
        variant = rs - r + (NA_WIN_ROWS - 1)
        qrow = pl.ds(pl.multiple_of(r * GRID_W, GRID_W), GRID_W)
        krow = pl.ds(pl.multiple_of(rs * GRID_W, GRID_W), kwin)
        for blk in range(NA_HEADS // NA_HEAD_BLOCK):
            sl = slice(blk * bw, (blk + 1) * bw)
            q_p, k_p, v_p = q_ref[0, qrow, sl], k_ref[0, krow, sl], v_ref[0, krow, sl]
            q_all = jnp.concatenate([jnp.where(head_of_lane == hh, q_p, jnp.zeros_like(q_p))
                                     for hh in range(NA_HEAD_BLOCK)], axis=0)
            logits = lax.dot_general(q_all, k_p, (((1,), (1,)), ((), ())), preferred_element_type=F32)
            logits = logits + bias_ref[variant, blk]
            e = jnp.exp(logits - jnp.max(logits, axis=-1, keepdims=True))
            den = jnp.sum(e, axis=-1, keepdims=True)
            o = jnp.dot(e.astype(BF16), v_p, preferred_element_type=F32) / den
            out = o[0:GRID_W]
            for hh in range(1, NA_HEAD_BLOCK):
                out = jnp.where(head_of_lane == hh, o[hh * GRID_W:(hh + 1) * GRID_W], out)
            o_ref[0, qrow, sl] = out.astype(o_ref.dtype)
        return carry

    lax.fori_loop(0, rows, row_fn, 0, unroll=2)


def _na_attention(z, bias):
    b, s, _ = z.shape
    width = NA_HEADS * NA_HEAD_DIM
    rows = s // GRID_W
    spec = lambda c: pl.BlockSpec((1, s, width), lambda i, c=c: (i, 0, c))
    return pl.pallas_call(
        functools.partial(_na_body, rows=rows),
        grid=(b,),
        in_specs=[spec(0), spec(1), spec(2), _const_spec(bias.shape)],
        out_specs=pl.BlockSpec((1, s, width), lambda i: (i, 0, 0)),
        out_shape=jax.ShapeDtypeStruct((b, s, width), BF16),
        compiler_params=_cparams(("parallel",)),
        name="na_attention",
    )(z, z, z, bias)


def _s5_matrices(lam_re, lam_im, log_dt, b_re, b_im, c_re, c_im, d_skip):
    L, hg, p = S5_CHUNK, S5_GROUP, S5_STATE
    g = lam_re.shape[1]
    gp = g // 2
    tau = jnp.arange(L + 1, dtype=F32)
    eye2 = jnp.eye(2, dtype=F32)
    pw, bbar_t, cc = [], [], []
    for d in range(2):
        lam = lax.complex(lam_re[d].astype(F32), lam_im[d].astype(F32))
        lam_dt = lam * jnp.exp(log_dt[d].astype(F32))[:, None]
        lam_bar = jnp.exp(lam_dt)
        pw.append(jnp.exp(lam_dt[None] * tau[:, None, None]))
        b_c = lax.complex(b_re[d].astype(F32), b_im[d].astype(F32))
        bbar_t.append((((lam_bar - 1.0) / lam)[:, :, None] * b_c).transpose(0, 2, 1))
        cc.append(lax.complex(c_re[d].astype(F32), c_im[d].astype(F32)))

    kf = jnp.real(jnp.einsum('gnp,tgp,ghp->tghn', cc[0], pw[0][:L], bbar_t[0]))
    kb = jnp.real(jnp.einsum('gnp,tgp,ghp->tghn', cc[1], pw[1][:L], bbar_t[1]))
    skip = jnp.eye(hg, dtype=F32)[None] * d_skip.astype(F32).reshape(g, 1, hg)
    tab = jnp.concatenate([kb[1:][::-1], (kf[0] + kb[0] + skip)[None], kf[1:]])
    tab = jnp.einsum('aqihn,ij->qihajn', tab.reshape(2 * L - 1, gp, 2, hg, hg), eye2)
    tab = tab.reshape(gp, 2 * hg, (2 * L - 1) * 2 * hg)
    lag_tab = jnp.pad(tab, ((0, 0), (0, 0), (0, 2 * hg)))

    a_f = pw[0][L - 1 - jnp.arange(L)][:, :, None, :] * bbar_t[0][None]
    a_b = pw[1][jnp.arange(L)][:, :, None, :] * bbar_t[1][None]
    loc = jnp.stack([jnp.real(a_f), jnp.imag(a_f), jnp.real(a_b), jnp.imag(a_b)])
    m_loc = jnp.einsum('ksqihp,ij->qsihkjp', loc.reshape(4, L, gp, 2, hg, p), eye2)
    m_loc = m_loc.reshape(gp, L * 2 * hg, 4 * 2 * p)

    e_f = cc[0][None] * pw[0][1 + jnp.arange(L)][:, :, None, :]
    e_b = cc[1][None] * pw[1][L - jnp.arange(L)][:, :, None, :]
    cr = jnp.stack([jnp.real(e_f), -jnp.imag(e_f), jnp.real(e_b), -jnp.imag(e_b)])
    m_cross_t = jnp.einsum('klqinp,ij->qlinkjp', cr.reshape(4, L, gp, 2, hg, p), eye2)
    m_cross_t = m_cross_t.reshape(gp, L * 2 * hg, 4 * 2 * p)

    lam_l = jnp.stack([jnp.real(pw[0][L]), jnp.imag(pw[0][L]),
                       jnp.real(pw[1][L]), jnp.imag(pw[1][L])])
    lam_l = lam_l.reshape(4, gp, 2 * p).transpose(1, 0, 2)
    return lag_tab, m_loc.astype(BF16), m_cross_t.astype(BF16), lam_l


def _s5_body(u_ref, tab_ref, mloc_ref, mcross_ref, lam_ref, y_ref, mintra_ref, xloc_ref, xin_ref, *,
             bsz, tr):
    rows, width = u_ref.shape[1], u_ref.shape[2]
    nc = rows // bsz
    w = lam_ref.shape[2]

    tab = tab_ref[0]
    for s in range(S5_CHUNK):
        off = (S5_CHUNK - 1 - s) * S5_SLOT
        mintra_ref[s * S5_SLOT:(s + 1) * S5_SLOT, :] = tab[:, off:off + width].astype(BF16)

    def loc_fn(i, carry):
        rs = pl.ds(pl.multiple_of(i * tr, tr), tr)
        xloc = jnp.dot(u_ref[0, rs, :], mloc_ref[0], preferred_element_type=F32)
        for part in range(4):
            xloc_ref[part, rs, :] = xloc[:, part * w:(part + 1) * w]
        return carry

    lax.fori_loop(0, rows // tr, loc_fn, 0)

    afr, afi, abr, abi = lam_ref[0, 0:1, :], lam_ref[0, 1:2, :], lam_ref[0, 2:3, :], lam_ref[0, 3:4, :]

    def carry_fn(i, state):
        sfr, sfi, sbr, sbi = state
        rf = pl.ds(pl.multiple_of(i * bsz, bsz), bsz)
        rb = pl.ds(pl.multiple_of((nc - 1 - i) * bsz, bsz), bsz)
        xin_ref[0, rf, :] = sfr
        xin_ref[1, rf, :] = sfi
        xin_ref[2, rb, :] = sbr
        xin_ref[3, rb, :] = sbi
        nfr = afr * sfr - afi * sfi + xloc_ref[0, rf, :]
        nfi = afr * sfi + afi * sfr + xloc_ref[1, rf, :]
        nbr = abr * sbr - abi * sbi + xloc_ref[2, rb, :]
        nbi = abr * sbi + abi * sbr + xloc_ref[3, rb, :]
        return nfr, nfi, nbr, nbi

    zero = jnp.zeros((bsz, w), F32)
    lax.fori_loop(0, nc, carry_fn, (zero, zero, zero, zero), unroll=2)

    def out_fn(i, carry):
        rs = pl.ds(pl.multiple_of(i * tr, tr), tr)
        y = jnp.dot(u_ref[0, rs, :], mintra_ref[...], preferred_element_type=F32)
        xin = jnp.concatenate([xin_ref[part, rs, :] for part in range(4)], axis=1)
        y = y + lax.dot_general(xin.astype(BF16), mcross_ref[0], (((1,), (1,)), ((), ())),
                                preferred_element_type=F32)
        y_ref[0, rs, :] = y
        return carry

    lax.fori_loop(0, rows // tr, out_fn, 0)


def _s5_scan(u_pairs, mats, bsz):
    lag_tab, mloc, mcross_t, lam_l = mats
    gp, rows, width = u_pairs.shape
    tr = min(512, rows)
    pair_spec = lambda a: pl.BlockSpec((1,) + a.shape[1:], lambda i: (i, 0, 0))
    return pl.pallas_call(
        functools.partial(_s5_body, bsz=bsz, tr=tr),
        grid=(gp,),
        in_specs=[pair_spec(u_pairs), pair_spec(lag_tab), pair_spec(mloc), pair_spec(mcross_t),
                  pair_spec(lam_l)],
        out_specs=pl.BlockSpec((1, rows, width), lambda i: (i, 0, 0)),
        out_shape=jax.ShapeDtypeStruct((gp, rows, width), F32),
        scratch_shapes=[pltpu.VMEM((width, width), BF16),
                        pltpu.VMEM((4, rows, lam_l.shape[2]), F32), pltpu.VMEM((4, rows, lam_l.shape[2]), F32)],
        compiler_params=_cparams(("parallel",)),
        name="s5_scan",
    )(u_pairs, lag_tab, mloc, mcross_t, lam_l)


def _ab_out_body(x_ref, a_ref, y_ref, wglu_ref, bglu_ref, wout_ref, o_ref, ys_ref):
    bsz, tt, d = x_ref.shape
    half = a_ref.shape[2]
    lane = lax.broadcasted_iota(jnp.int32, (bsz, LANES), 1)
    for cl in range(tt // S5_CHUNK):
        for v in range(S5_CHUNK // S5_SLOTS):
            for w in range(half // LANES):
                outs = _slot_transpose4([y_ref[S5_SLOTS * w + j, cl, :, v * LANES:(v + 1) * LANES]
                                         for j in range(S5_SLOTS)], lane)
                pos = cl * S5_CHUNK + S5_SLOTS * v
                for i in range(S5_SLOTS):
                    ys_ref[w, pl.ds(pos + i, bsz, stride=tt), :] = outs[i]
    yg = _gelu(jnp.concatenate([ys_ref[w] for w in range(half // LANES)], axis=1))
    gate = _sigmoid(jnp.dot(yg.astype(BF16), wglu_ref[...], preferred_element_type=F32) + bglu_ref[...])
    b_out = (yg * gate).astype(BF16)
    a = a_ref[...].reshape(bsz * tt, half)
    acc = x_ref[...].reshape(bsz * tt, d) + jnp.dot(a, wout_ref[0:half, :], preferred_element_type=F32)
    acc = acc + jnp.dot(b_out, wout_ref[half:, :], preferred_element_type=F32)
    o_ref[...] = acc.reshape(bsz, tt, d)


def _ab_out(x, a, y_pairs, wglu, bglu, wout):
    bsz, seq, d = x.shape
    half = a.shape[2]
    pairs = y_pairs.shape[0]
    tt = AB_TOKENS
    tok = lambda n: pl.BlockSpec((bsz, tt, n), lambda i: (0, i, 0))
    return pl.pallas_call(
        _ab_out_body,
        grid=(seq // tt,),
        in_specs=[tok(d), tok(half),
                  pl.BlockSpec((pairs, tt // S5_CHUNK, bsz, half), lambda i: (0, i, 0, 0)),
                  _const_spec(wglu.shape), _const_spec((1, half)), _const_spec(wout.shape)],
        out_specs=tok(d),
        out_shape=jax.ShapeDtypeStruct((bsz, seq, d), F32),
        scratch_shapes=[pltpu.VMEM((half // LANES, bsz * tt, LANES), F32)],
        compiler_params=_cparams(("parallel",)),
        name="ab_out_proj",
    )(x, a, y_pairs, wglu, bglu.reshape(1, half), wout)


def _ffn_body(x_ref, xp_ref, xn_ref, g_ref, wup_ref, cw_ref, cb_ref, wdn_ref, p_ref, pg_ref, wg_ref,
              wp_ref, fg_ref, o_ref, h_ref, act_ref, *, tiles_per_seq, tf, final):
    tm = x_ref.shape[0]
    dff = wdn_ref.shape[0]
    ext = tm + 2 * CONV_HALO
    i = pl.program_id(0)
    has_prev = (i % tiles_per_seq != 0).astype(F32)
    has_next = (i % tiles_per_seq != tiles_per_seq - 1).astype(F32)
    g = g_ref[...]
    h_ref[0:tm, :] = _rms(x_ref[...], g).astype(BF16)
    h_ref[tm:tm + CONV_HALO, :] = (_rms(xn_ref[...], g) * has_next).astype(BF16)
    h_ref[tm + CONV_HALO:ext, :] = (_rms(xp_ref[...], g) * has_prev).astype(BF16)
    h = h_ref[...]

    def conv(u, c0):
        w = cw_ref[:, c0:c0 + tf]
        out = (pltpu.roll(u, 1, 0) * w[0:1] + u * w[1:2] + pltpu.roll(u, ext - 1, 0) * w[2:3]
               + cb_ref[:, c0:c0 + tf])
        return out[0:tm]

    for j in range(dff // tf):
        ua = jnp.dot(h, wup_ref[:, j * tf:(j + 1) * tf], preferred_element_type=F32)
        ug = jnp.dot(h, wup_ref[:, dff + j * tf:dff + (j + 1) * tf], preferred_element_type=F32)
        act_ref[:, j * tf:(j + 1) * tf] = (_gelu(conv(ug, dff + j * tf)) * conv(ua, j * tf)).astype(BF16)
    x = x_ref[...] + jnp.dot(act_ref[...], wdn_ref[...], preferred_element_type=F32)
    gate = _sigmoid(jnp.dot(_rms(x, pg_ref[...]).astype(BF16), wg_ref[...], preferred_element_type=F32))
    out = x + gate * jnp.dot(p_ref[...].astype(BF16), wp_ref[...], preferred_element_type=F32)
    if final:
        out = _rms(out, fg_ref[...])
    o_ref[...] = out


def _ffn_ple(x, seq, g, wup, cw, cb, wdn, p, layer, pg, wg, wp, fg, final, tm=512, tf=256):
    m, d = x.shape
    dff = wdn.shape[0]
    pd = p.shape[2]
    hb = tm // CONV_HALO
    nblk = m // CONV_HALO
    return pl.pallas_call(
        functools.partial(_ffn_body, tiles_per_seq=seq // tm, tf=tf, final=final),
        grid=(m // tm,),
        in_specs=[pl.BlockSpec((tm, d), lambda i: (i, 0)),
                  pl.BlockSpec((CONV_HALO, d), lambda i: (jnp.maximum(i * hb - 1, 0), 0)),
                  pl.BlockSpec((CONV_HALO, d), lambda i: (jnp.minimum((i + 1) * hb, nblk - 1), 0)),
                  _const_spec((1, d)), _const_spec(wup.shape), _const_spec(cw.shape),
                  _const_spec((1, 2 * dff)), _const_spec(wdn.shape),
                  pl.BlockSpec((None, tm, pd), lambda i: (layer, i, 0)),
                  _const_spec((1, d)), _const_spec(wg.shape), _const_spec(wp.shape), _const_spec((1, d))],
        out_specs=pl.BlockSpec((tm, d), lambda i: (i, 0)),
        out_shape=jax.ShapeDtypeStruct((m, d), F32),
        scratch_shapes=[pltpu.VMEM((tm + 2 * CONV_HALO, d), BF16), pltpu.VMEM((tm, dff), BF16)],
        compiler_params=_cparams(("parallel",)),
        name="ffn_ple",
    )(x, x, x, g.reshape(1, d), wup, cw, cb.reshape(1, 2 * dff), wdn, p, pg.reshape(1, d), wg, wp,
      fg.reshape(1, d))


def _ret_in_body(x_ref, g_ref, w_ref, cos_ref, sin_ref, q_ref, k_ref, v_ref, gt_ref):
    h = _rms(x_ref[...], g_ref[...]).astype(BF16)
    cos, sin = cos_ref[...], sin_ref[...]
    half = RET_QK // 2
    dq = q_ref.shape[1]
    dv = v_ref.shape[1]

    def rot(z, scale):
        x1, x2 = z[:, :half], z[:, half:]
        return jnp.concatenate([(x1 * cos - x2 * sin) * scale, (x1 * sin + x2 * cos) * scale], axis=-1)

    for j in range(dq // RET_QK):
        c0 = j * RET_QK
        zq = jnp.dot(h, w_ref[:, c0:c0 + RET_QK], preferred_element_type=F32)
        q_ref[:, c0:c0 + RET_QK] = rot(zq, 1.0).astype(BF16)
        zk = jnp.dot(h, w_ref[:, dq + c0:dq + c0 + RET_QK], preferred_element_type=F32)
        k_ref[:, c0:c0 + RET_QK] = rot(zk, RET_QK ** -0.5).astype(BF16)
    tn = 512
    for j in range(dv // tn):
        c0 = j * tn
        v_ref[:, c0:c0 + tn] = jnp.dot(h, w_ref[:, 2 * dq + c0:2 * dq + c0 + tn],
                                       preferred_element_type=F32).astype(BF16)
        gt_ref[:, c0:c0 + tn] = jnp.dot(h, w_ref[:, 2 * dq + dv + c0:2 * dq + dv + c0 + tn],
                                        preferred_element_type=F32).astype(BF16)


def _ret_in_proj(x, seq, g, w, cos, sin, tm=512):
    m, d = x.shape
    dq = RET_HEADS * RET_QK
    dv = RET_HEADS * RET_V
    tps = seq // tm
    row = lambda n: pl.BlockSpec((tm, n), lambda i: (i, 0))
    pos = pl.BlockSpec((tm, RET_QK // 2), lambda i: (i % tps, 0))
    return pl.pallas_call(
        _ret_in_body,
        grid=(m // tm,),
        in_specs=[row(d), _const_spec((1, d)), _const_spec(w.shape), pos, pos],
        out_specs=[row(dq), row(dq), row(dv), row(dv)],
        out_shape=[jax.ShapeDtypeStruct((m, dq), BF16), jax.ShapeDtypeStruct((m, dq), BF16),
                   jax.ShapeDtypeStruct((m, dv), BF16), jax.ShapeDtypeStruct((m, dv), BF16)],
        compiler_params=_cparams(("parallel",)),
        name="ret_in_proj",
    )(x, g.reshape(1, d), w, cos, sin)


def _ret_tables(decay_param):
    L = RET_CHUNK
    lg = -jnp.exp(decay_param.astype(F32))
    lg_f, lg_b = lg[0][:, None], lg[1][:, None]
    pos = jnp.arange(L, dtype=F32)
    diff = pos[:, None] - pos[None, :]
    dmat = jnp.where(diff >= 0, jnp.exp(lg_f[:, :, None] * jnp.abs(diff)),
                     jnp.exp(lg_b[:, :, None] * jnp.abs(diff)))
    vecs = jnp.stack([jnp.exp(lg_f * (pos + 1.0)),
                      jnp.exp(lg_f * (L - 1.0 - pos)),
                      jnp.exp(lg_b * (L - pos)),
                      jnp.exp(lg_b * pos)], axis=-1)
    cdec = jnp.exp(lg * L).T
    return dmat, vecs, cdec


def _ret_body(cdec_ref, q_ref, k_ref, v_ref, gt_ref, dmat_ref, vec_ref, o_ref, sb_ref, st_ref):
    L = RET_CHUNK
    seq = q_ref.shape[1]
    n = seq // L
    hd = pl.program_id(1)
    cf = cdec_ref[hd, 0]
    cb = cdec_ref[hd, 1]
    vec = vec_ref[0]
    qf, kf, qb, kb = vec[:, 0:1], vec[:, 1:2], vec[:, 2:3], vec[:, 3:4]
    contract0 = (((0,), (0,)), ((), ()))

    def scaled(t, col):
        return (t.astype(F32) * col).astype(BF16)

    st_ref[...] = jnp.zeros_like(st_ref)

    def bwd(i, carry):
        c = n - 1 - i
        rs = pl.ds(pl.multiple_of(c * L, L), L)
        sb_ref[c] = st_ref[...].astype(BF16)
        st_ref[...] = cb * st_ref[...] + lax.dot_general(scaled(k_ref[0, rs, :], kb), v_ref[0, rs, :],
                                                         contract0, preferred_element_type=F32)
        return carry

    lax.fori_loop(0, n, bwd, 0, unroll=2)

    st_ref[...] = jnp.zeros_like(st_ref)

    def fwd(c, carry):
        rs = pl.ds(pl.multiple_of(c * L, L), L)
        q_c, k_c, v_c = q_ref[0, rs, :], k_ref[0, rs, :], v_ref[0, rs, :]
        scores = lax.dot_general(q_c, k_c, (((1,), (1,)), ((), ())), preferred_element_type=F32)
        y = jnp.dot((scores * dmat_ref[0]).astype(BF16), v_c, preferred_element_type=F32)
        y = y + jnp.dot(scaled(q_c, qf), st_ref[...].astype(BF16), preferred_element_type=F32)
        y = y + jnp.dot(scaled(q_c, qb), sb_ref[c], preferred_element_type=F32)
        st_ref[...] = cf * st_ref[...] + lax.dot_general(scaled(k_c, kf), v_c, contract0,
                                                         preferred_element_type=F32)
        y = y * lax.rsqrt(jnp.mean(y * y, axis=-1, keepdims=True) + EPS)
        gt = gt_ref[0, rs, :].astype(F32)
        o_ref[0, rs, :] = (gt * _sigmoid(gt) * y).astype(o_ref.dtype)
        return carry

    lax.fori_loop(0, n, fwd, 0, unroll=2)


def _retention(q, k, v, gt, tables):
    dmat, vecs, cdec = tables
    b, s, _ = q.shape
    L = RET_CHUNK
    qk_spec = pl.BlockSpec((1, s, RET_QK), lambda i, h, *_: (i, 0, h))
    v_spec = pl.BlockSpec((1, s, RET_V), lambda i, h, *_: (i, 0, h))
    grid_spec = pltpu.PrefetchScalarGridSpec(
        num_scalar_prefetch=1,
        grid=(b, RET_HEADS),
        in_specs=[qk_spec, qk_spec, v_spec, v_spec,
                  pl.BlockSpec((1, L, L), lambda i, h, *_: (h, 0, 0)),
                  pl.BlockSpec((1, L, 4), lambda i, h, *_: (h, 0, 0))],
        out_specs=v_spec,
        scratch_shapes=[pltpu.VMEM((s // L, RET_QK, RET_V), BF16), pltpu.VMEM((RET_QK, RET_V), F32)],
    )
    return pl.pallas_call(
        _ret_body,
        grid_spec=grid_spec,
        out_shape=jax.ShapeDtypeStruct((b, s, RET_HEADS * RET_V), BF16),
        compiler_params=_cparams(("parallel", "parallel")),
        name="retention",
    )(cdec, q, k, v, gt, dmat, vecs)


def _proj_residual_body(x_ref, y_ref, w_ref, o_ref):
    o_ref[...] = x_ref[...] + jnp.dot(y_ref[...], w_ref[...], preferred_element_type=F32)


def _proj_residual(x, y, w, tm=512):
    m, d = x.shape
    kd = y.shape[1]
    return pl.pallas_call(
        _proj_residual_body,
        grid=(m // tm,),
        in_specs=[pl.BlockSpec((tm, d), lambda i: (i, 0)), pl.BlockSpec((tm, kd), lambda i: (i, 0)),
                  _const_spec(w.shape)],
        out_specs=pl.BlockSpec((tm, d), lambda i: (i, 0)),
        out_shape=jax.ShapeDtypeStruct((m, d), F32),
        compiler_params=_cparams(("parallel",)),
        name="ret_out_proj",
    )(x, y, w)


def _na_s5_layer(x, bsz, seq, norm, w_in, rpb, lam_re, lam_im, log_dt, b_re, b_im, c_re, c_im, d_skip,
                 w_glu, b_glu, w_out):
    m, d = x.shape
    half = NA_HEADS * NA_HEAD_DIM
    col_scale = jnp.concatenate([jnp.full((half,), NA_HEAD_DIM ** -0.5, F32), jnp.ones((3 * half,), F32)])
    z, u_pairs = _ab_in_proj(x.reshape(bsz, seq, d), norm, (w_in * col_scale).astype(BF16), 3 * half)
    a_out = _na_attention(z, _na_bias_table(rpb))
    mats = _s5_matrices(lam_re, lam_im, log_dt, b_re, b_im, c_re, c_im, d_skip)
    pairs, nc = u_pairs.shape[0], u_pairs.shape[1]
    y_pairs = _s5_scan(u_pairs.reshape(pairs, nc * bsz, half), mats, bsz).reshape(pairs, nc, bsz, half)
    out = _ab_out(x.reshape(bsz, seq, d), a_out, y_pairs, w_glu.astype(BF16), b_glu, w_out.astype(BF16))
    return out.reshape(m, d)


def _rope_tables(seq):
    half = RET_QK // 2
    inv_freq = ROPE_BASE ** (-jnp.arange(half, dtype=F32) / half)
    ang = jnp.arange(seq, dtype=F32)[:, None] * inv_freq[None, :]
    return jnp.cos(ang), jnp.sin(ang)


def _retention_layer(x, bsz, seq, norm, w_in, decay, w_out):
    m, d = x.shape
    cos, sin = _rope_tables(seq)
    q, k, v, gt = _ret_in_proj(x, seq, norm, w_in.astype(BF16), cos, sin)
    shp = lambda t: t.reshape(bsz, seq, t.shape[-1])
    y = _retention(shp(q), shp(k), shp(v), shp(gt), _ret_tables(decay))
    return _proj_residual(x, y.reshape(m, -1), w_out.astype(BF16))


def kernel(x, p, ab_norm, ab_w_in, na_rpb, s5_lambda_re, s5_lambda_im, s5_log_dt, s5_b_re, s5_b_im,
           s5_c_re, s5_c_im, s5_d, s5_w_glu, s5_b_glu, ab_w_out, ret_norm, ret_w_in, ret_decay, ret_w_out,
           ffn_norm, ffn_w_up, ffn_conv_w, ffn_conv_b, ffn_w_down, ple_norm, ple_w_gate, ple_w_proj,
           final_norm):
    bsz, seq, d = x.shape
    depth = p.shape[0]
    m = bsz * seq
    xs = x.reshape(m, d)
    p_tok = p.reshape(depth, m, p.shape[-1])
    for i in range(depth):
        j = i // 2
        if i % 2 == 0:
            xs = _na_s5_layer(xs, bsz, seq, ab_norm[j], ab_w_in[j], na_rpb[j], s5_lambda_re[j],
                              s5_lambda_im[j], s5_log_dt[j], s5_b_re[j], s5_b_im[j], s5_c_re[j],
                              s5_c_im[j], s5_d[j], s5_w_glu[j], s5_b_glu[j], ab_w_out[j])
        else:
            xs = _retention_layer(xs, bsz, seq, ret_norm[j], ret_w_in[j], ret_decay[j], ret_w_out[j])
        xs = _ffn_ple(xs, seq, ffn_norm[i], ffn_w_up[i].astype(BF16), ffn_conv_w[i], ffn_conv_b[i],
                      ffn_w_down[i].astype(BF16), p_tok, i, ple_norm[i], ple_w_gate[i].astype(BF16),
                      ple_w_proj[i].astype(BF16), final_norm, final=(i == depth - 1))
    return xs.reshape(bsz, seq, d)
```

```python
import functools

import jax
import jax.numpy as jnp
from jax import lax
from jax.experimental import pallas as pl
from jax.experimental.pallas import tpu as pltpu

F32 = jnp.float32
BF16 = jnp.bfloat16

LANES = 128
EPS = 1e-6
GRID_W = 64
NA_HEADS = 8
NA_HEAD_DIM = 64
NA_WIN_ROWS = 8
NA_WIN_COLS = 16
NA_MASK = -1e30
LOG2E = 1.4426950408889634
NA_HEAD_BLOCK = 4
S5_GROUP = 16
S5_STATE = 64
S5_CHUNK = 16
S5_SLOT = 2 * S5_GROUP
S5_SLOTS = LANES // S5_SLOT
AB_TOKENS = 32
RET_HEADS = 4
RET_QK = 256
RET_V = 512
RET_CHUNK = 256
ROPE_BASE = 10000.0
CONV_HALO = 8
VMEM_LIMIT = 56 * 1024 * 1024


def _cparams(sem):
    return pltpu.CompilerParams(dimension_semantics=sem, vmem_limit_bytes=VMEM_LIMIT)


def _const_spec(shape):
    nd = len(shape)
    return pl.BlockSpec(shape, lambda *_: (0,) * nd, pipeline_mode=pl.Buffered(1))


def _rms(xf, g):
    ms = jnp.mean(xf * xf, axis=-1, keepdims=True)
    return xf * lax.rsqrt(ms + EPS) * g


def _gelu(x):
    return 0.5 * x * (1.0 + lax.erf(x * (2.0 ** -0.5)))


def _sigmoid(x):
    return 0.5 * jnp.tanh(0.5 * x) + 0.5


def _slot_transpose4(a, lane):
    lo = lane < 2 * S5_SLOT
    even = (lane % (2 * S5_SLOT)) < S5_SLOT
    b0 = jnp.where(lo, a[0], pltpu.roll(a[2], 2 * S5_SLOT, 1))
    b2 = jnp.where(lo, pltpu.roll(a[0], 2 * S5_SLOT, 1), a[2])
    b1 = jnp.where(lo, a[1], pltpu.roll(a[3], 2 * S5_SLOT, 1))
    b3 = jnp.where(lo, pltpu.roll(a[1], 2 * S5_SLOT, 1), a[3])
    return [jnp.where(even, b0, pltpu.roll(b1, S5_SLOT, 1)),
            jnp.where(even, pltpu.roll(b0, LANES - S5_SLOT, 1), b1),
            jnp.where(even, b2, pltpu.roll(b3, S5_SLOT, 1)),
            jnp.where(even, pltpu.roll(b2, LANES - S5_SLOT, 1), b3)]


def _ab_in_body(x_ref, g_ref, w_ref, z_ref, u_ref, zs_ref, *, tn):
    bsz, tt, d = x_ref.shape
    h = _rms(x_ref[...].reshape(bsz * tt, d), g_ref[...]).astype(BF16)
    nz = z_ref.shape[2]
    for j in range(nz // tn):
        z_ref[:, :, j * tn:(j + 1) * tn] = jnp.dot(
            h, w_ref[:, j * tn:(j + 1) * tn], preferred_element_type=F32
        ).astype(z_ref.dtype).reshape(bsz, tt, tn)
    zu = jnp.dot(h, w_ref[:, nz:], preferred_element_type=F32)
    for w in range(zs_ref.shape[0]):
        zs_ref[w] = zu[:, w * LANES:(w + 1) * LANES]
    lane = lax.broadcasted_iota(jnp.int32, (bsz, LANES), 1)
    for cl in range(tt // S5_CHUNK):
        for v in range(S5_CHUNK // S5_SLOTS):
            for w in range(zs_ref.shape[0]):
                pos = cl * S5_CHUNK + S5_SLOTS * v
                outs = _slot_transpose4([zs_ref[w, pl.ds(pos + i, bsz, stride=tt), :]
                                         for i in range(S5_SLOTS)], lane)
                for j in range(S5_SLOTS):
                    u_ref[S5_SLOTS * w + j, cl, :, v * LANES:(v + 1) * LANES] = outs[j].astype(u_ref.dtype)


def _ab_in_proj(x, g, w, nz, tn=512):
    bsz, seq, d = x.shape
    nu = w.shape[1] - nz
    pairs = nu // S5_SLOT
    assert S5_CHUNK * S5_SLOT == nu and pairs == (nu // LANES) * S5_SLOTS
    tt = AB_TOKENS
    return pl.pallas_call(
        functools.partial(_ab_in_body, tn=tn),
        grid=(seq // tt,),
        in_specs=[pl.BlockSpec((bsz, tt, d), lambda i: (0, i, 0)), _const_spec((1, d)), _const_spec(w.shape)],
        out_specs=[pl.BlockSpec((bsz, tt, nz), lambda i: (0, i, 0)),
                   pl.BlockSpec((pairs, tt // S5_CHUNK, bsz, nu), lambda i: (0, i, 0, 0))],
        out_shape=[jax.ShapeDtypeStruct((bsz, seq, nz), BF16),
                   jax.ShapeDtypeStruct((pairs, seq // S5_CHUNK, bsz, nu), BF16)],
        scratch_shapes=[pltpu.VMEM((nu // LANES, bsz * tt, LANES), F32)],
        compiler_params=_cparams(("parallel",)),
        name="ab_in_proj",
    )(x, g.reshape(1, d), w)


def _na_bias_table(rpb):
    cols = jnp.arange(GRID_W)
    cs = jnp.clip(cols - NA_WIN_COLS // 2, 0, GRID_W - NA_WIN_COLS)
    j = jnp.arange(GRID_W)
    inwin = (j[None, :] >= cs[:, None]) & (j[None, :] < cs[:, None] + NA_WIN_COLS)
    dc = jnp.clip(j[None, :] - cols[:, None] + NA_WIN_COLS - 1, 0, 2 * NA_WIN_COLS - 2)
    t = (rpb.astype(F32) * LOG2E)[:, :, dc]
    t = jnp.where(inwin[None, None], t, NA_MASK).transpose(0, 2, 1, 3)
    h = rpb.shape[0]
    kw = NA_WIN_ROWS * GRID_W
    return jnp.stack([t[:, :, v:v + NA_WIN_ROWS, :].reshape(h // NA_HEAD_BLOCK, NA_HEAD_BLOCK * GRID_W, kw)
                      for v in range(NA_WIN_ROWS)])


def _na_body(q_ref, k_ref, v_ref, bias_ref, o_ref, *, rows):
    kwin = NA_WIN_ROWS * GRID_W
    bw = NA_HEAD_BLOCK * NA_HEAD_DIM
    head_of_lane = lax.broadcasted_iota(jnp.int32, (GRID_W, bw), 1) // NA_HEAD_DIM

    def row_fn(r, carry):
        rs = jnp.clip(r - NA_WIN_ROWS // 2, 0, rows - NA_WIN_ROWS)
        variant = rs - r + (NA_WIN_ROWS - 1)
        qrow = pl.ds(pl.multiple_of(r * GRID_W, GRID_W), GRID_W)
        krow = pl.ds(pl.multiple_of(rs * GRID_W, GRID_W), kwin)
        for blk in range(NA_HEADS // NA_HEAD_BLOCK):
            sl = slice(blk * bw, (blk + 1) * bw)
            q_p, k_p, v_p = q_ref[0, qrow, sl], k_ref[0, krow, sl], v_ref[0, krow, sl]
            q_all = jnp.concatenate([jnp.where(head_of_lane == hh, q_p, jnp.zeros_like(q_p))
                                     for hh in range(NA_HEAD_BLOCK)], axis=0)
            logits = lax.dot_general(q_all, k_p, (((1,), (1,)), ((), ())), preferred_element_type=F32)
            logits = logits + bias_ref[variant, blk]
            e = jnp.exp2(logits - jnp.max(logits, axis=-1, keepdims=True))
            den = jnp.sum(e, axis=-1, keepdims=True)
            o = jnp.dot(e.astype(BF16), v_p, preferred_element_type=F32) / den
            out = o[0:GRID_W]
            for hh in range(1, NA_HEAD_BLOCK):
                out = jnp.where(head_of_lane == hh, o[hh * GRID_W:(hh + 1) * GRID_W], out)
            o_ref[0, qrow, sl] = out.astype(o_ref.dtype)
        return carry

    lax.fori_loop(0, rows, row_fn, 0, unroll=2)


def _na_attention(z, bias):
    b, s, _ = z.shape
    width = NA_HEADS * NA_HEAD_DIM
    rows = s // GRID_W
    spec = lambda c: pl.BlockSpec((1, s, width), lambda i, c=c: (i, 0, c))
    return pl.pallas_call(
        functools.partial(_na_body, rows=rows),
        grid=(b,),
        in_specs=[spec(0), spec(1), spec(2), _const_spec(bias.shape)],
        out_specs=pl.BlockSpec((1, s, width), lambda i: (i, 0, 0)),
        out_shape=jax.ShapeDtypeStruct((b, s, width), BF16),
        compiler_params=_cparams(("parallel",)),
        name="na_attention",
    )(z, z, z, bias)


def _s5_matrices(lam_re, lam_im, log_dt, b_re, b_im, c_re, c_im, d_skip):
    L, hg, p = S5_CHUNK, S5_GROUP, S5_STATE
    g = lam_re.shape[1]
    gp = g // 2
    tau = jnp.arange(L + 1, dtype=F32)
    eye2 = jnp.eye(2, dtype=F32)
    pw, bbar_t, cc = [], [], []
    for d in range(2):
        lam = lax.complex(lam_re[d].astype(F32), lam_im[d].astype(F32))
        lam_dt = lam * jnp.exp(log_dt[d].astype(F32))[:, None]
        lam_bar = jnp.exp(lam_dt)
        pw.append(jnp.exp(lam_dt[None] * tau[:, None, None]))
        b_c = lax.complex(b_re[d].astype(F32), b_im[d].astype(F32))
        bbar_t.append((((lam_bar - 1.0) / lam)[:, :, None] * b_c).transpose(0, 2, 1))
        cc.append(lax.complex(c_re[d].astype(F32), c_im[d].astype(F32)))

    kf = jnp.real(jnp.einsum('gnp,tgp,ghp->tghn', cc[0], pw[0][:L], bbar_t[0]))
    kb = jnp.real(jnp.einsum('gnp,tgp,ghp->tghn', cc[1], pw[1][:L], bbar_t[1]))
    skip = jnp.eye(hg, dtype=F32)[None] * d_skip.astype(F32).reshape(g, 1, hg)
    tab = jnp.concatenate([kb[1:][::-1], (kf[0] + kb[0] + skip)[None], kf[1:]])
    tab = jnp.einsum('aqihn,ij->qihajn', tab.reshape(2 * L - 1, gp, 2, hg, hg), eye2)
    tab = tab.reshape(gp, 2 * hg, (2 * L - 1) * 2 * hg)
    lag_tab = jnp.pad(tab, ((0, 0), (0, 0), (0, 2 * hg)))

    a_f = pw[0][L - 1 - jnp.arange(L)][:, :, None, :] * bbar_t[0][None]
    a_b = pw[1][jnp.arange(L)][:, :, None, :] * bbar_t[1][None]
    loc = jnp.stack([jnp.real(a_f), jnp.imag(a_f), jnp.real(a_b), jnp.imag(a_b)])

    e_f = cc[0][None] * pw[0][1 + jnp.arange(L)][:, :, None, :]
    e_b = cc[1][None] * pw[1][L - jnp.arange(L)][:, :, None, :]
    cr = jnp.stack([jnp.real(e_f), -jnp.imag(e_f), jnp.real(e_b), -jnp.imag(e_b)])

    lam_l = jnp.stack([jnp.real(pw[0][L]), jnp.imag(pw[0][L]),
                       jnp.real(pw[1][L]), jnp.imag(pw[1][L])])
    lam_l = lam_l.reshape(4, gp, 2 * p).transpose(1, 0, 2)
    return lag_tab, loc.reshape(4, L, gp, 2 * hg, p), cr.reshape(4, L, gp, 2 * hg, p), lam_l


def _s5_body(u_ref, tab_ref, loc_ref, cross_ref, lam_ref, y_ref, mintra_ref, mloc_ref, mcross_ref,
             xloc_ref, xin_ref, *, bsz, tr):
    rows, width = u_ref.shape[1], u_ref.shape[2]
    nc = rows // bsz
    w = lam_ref.shape[2]

    tab = tab_ref[0]
    for s in range(S5_CHUNK):
        off = (S5_CHUNK - 1 - s) * S5_SLOT
        mintra_ref[s * S5_SLOT:(s + 1) * S5_SLOT, :] = tab[:, off:off + width].astype(BF16)
    own = (lax.broadcasted_iota(jnp.int32, (S5_SLOT, w), 0) // S5_GROUP
           == lax.broadcasted_iota(jnp.int32, (S5_SLOT, w), 1) // S5_STATE)
    for part in range(4):
        for s in range(S5_CHUNK):
            dst = (slice(s * S5_SLOT, (s + 1) * S5_SLOT), slice(part * w, (part + 1) * w))
            for tab_ref_, m_ref in ((loc_ref, mloc_ref), (cross_ref, mcross_ref)):
                blk = tab_ref_[part, s]
                m_ref[dst] = jnp.where(own, jnp.concatenate([blk, blk], axis=1), 0.0).astype(BF16)

    def loc_fn(i, carry):
        rs = pl.ds(pl.multiple_of(i * tr, tr), tr)
        xloc = jnp.dot(u_ref[0, rs, :], mloc_ref[...], preferred_element_type=F32)
        for part in range(4):
            xloc_ref[part, rs, :] = xloc[:, part * w:(part + 1) * w]
        return carry

    lax.fori_loop(0, rows // tr, loc_fn, 0)

    afr, afi, abr, abi = lam_ref[0, 0:1, :], lam_ref[0, 1:2, :], lam_ref[0, 2:3, :], lam_ref[0, 3:4, :]

    def carry_fn(i, state):
        sfr, sfi, sbr, sbi = state
        rf = pl.ds(pl.multiple_of(i * bsz, bsz), bsz)
        rb = pl.ds(pl.multiple_of((nc - 1 - i) * bsz, bsz), bsz)
        xin_ref[0, rf, :] = sfr
        xin_ref[1, rf, :] = sfi
        xin_ref[2, rb, :] = sbr
        xin_ref[3, rb, :] = sbi
        nfr = afr * sfr - afi * sfi + xloc_ref[0, rf, :]
        nfi = afr * sfi + afi * sfr + xloc_ref[1, rf, :]
        nbr = abr * sbr - abi * sbi + xloc_ref[2, rb, :]
        nbi = abr * sbi + abi * sbr + xloc_ref[3, rb, :]
        return nfr, nfi, nbr, nbi

    zero = jnp.zeros((bsz, w), F32)
    lax.fori_loop(0, nc, carry_fn, (zero, zero, zero, zero), unroll=2)

    def out_fn(i, carry):
        rs = pl.ds(pl.multiple_of(i * tr, tr), tr)
        y = jnp.dot(u_ref[0, rs, :], mintra_ref[...], preferred_element_type=F32)
        xin = jnp.concatenate([xin_ref[part, rs, :] for part in range(4)], axis=1)
        y = y + lax.dot_general(xin.astype(BF16), mcross_ref[...], (((1,), (1,)), ((), ())),
                                preferred_element_type=F32)
        y_ref[0, rs, :] = y
        return carry

    lax.fori_loop(0, rows // tr, out_fn, 0)


def _s5_scan(u_pairs, mats, bsz):
    lag_tab, loc_tab, cross_tab, lam_l = mats
    gp, rows, width = u_pairs.shape
    tr = min(512, rows)
    w = lam_l.shape[2]
    pair_spec = lambda a: pl.BlockSpec((1,) + a.shape[1:], lambda i: (i, 0, 0))
    state_spec = pl.BlockSpec(loc_tab.shape[:2] + (None,) + loc_tab.shape[3:], lambda i: (0, 0, i, 0, 0))
    return pl.pallas_call(
        functools.partial(_s5_body, bsz=bsz, tr=tr),
        grid=(gp,),
        in_specs=[pair_spec(u_pairs), pair_spec(lag_tab), state_spec, state_spec, pair_spec(lam_l)],
        out_specs=pl.BlockSpec((1, rows, width), lambda i: (i, 0, 0)),
        out_shape=jax.ShapeDtypeStruct((gp, rows, width), F32),
        scratch_shapes=[pltpu.VMEM((width, width), BF16), pltpu.VMEM((width, 4 * w), BF16),
                        pltpu.VMEM((width, 4 * w), BF16),
                        pltpu.VMEM((4, rows, w), F32), pltpu.VMEM((4, rows, w), F32)],
        compiler_params=_cparams(("parallel",)),
        name="s5_scan",
    )(u_pairs, lag_tab, loc_tab, cross_tab, lam_l)


def _ab_out_body(x_ref, a_ref, y_ref, wglu_ref, bglu_ref, wout_ref, o_ref, ys_ref):
    bsz, tt, d = x_ref.shape
    half = a_ref.shape[2]
    lane = lax.broadcasted_iota(jnp.int32, (bsz, LANES), 1)
    for cl in range(tt // S5_CHUNK):
        for v in range(S5_CHUNK // S5_SLOTS):
            for w in range(half // LANES):
                outs = _slot_transpose4([y_ref[S5_SLOTS * w + j, cl, :, v * LANES:(v + 1) * LANES]
                                         for j in range(S5_SLOTS)], lane)
                pos = cl * S5_CHUNK + S5_SLOTS * v
                for i in range(S5_SLOTS):
                    ys_ref[w, pl.ds(pos + i, bsz, stride=tt), :] = outs[i]
    yg = _gelu(jnp.concatenate([ys_ref[w] for w in range(half // LANES)], axis=1))
    gate = _sigmoid(jnp.dot(yg.astype(BF16), wglu_ref[...], preferred_element_type=F32) + bglu_ref[...])
    b_out = (yg * gate).astype(BF16)
    a = a_ref[...].reshape(bsz * tt, half)
    acc = x_ref[...].reshape(bsz * tt, d) + jnp.dot(a, wout_ref[0:half, :], preferred_element_type=F32)
    acc = acc + jnp.dot(b_out, wout_ref[half:, :], preferred_element_type=F32)
    o_ref[...] = acc.reshape(bsz, tt, d)


def _ab_out(x, a, y_pairs, wglu, bglu, wout):
    bsz, seq, d = x.shape
    half = a.shape[2]
    pairs = y_pairs.shape[0]
    tt = AB_TOKENS
    tok = lambda n: pl.BlockSpec((bsz, tt, n), lambda i: (0, i, 0))
    return pl.pallas_call(
        _ab_out_body,
        grid=(seq // tt,),
        in_specs=[tok(d), tok(half),
                  pl.BlockSpec((pairs, tt // S5_CHUNK, bsz, half), lambda i: (0, i, 0, 0)),
                  _const_spec(wglu.shape), _const_spec((1, half)), _const_spec(wout.shape)],
        out_specs=tok(d),
        out_shape=jax.ShapeDtypeStruct((bsz, seq, d), F32),
        scratch_shapes=[pltpu.VMEM((half // LANES, bsz * tt, LANES), F32)],
        compiler_params=_cparams(("parallel",)),
        name="ab_out_proj",
    )(x, a, y_pairs, wglu, bglu.reshape(1, half), wout)


def _ffn_body(x_ref, xp_ref, xn_ref, g_ref, wup_ref, cw_ref, cb_ref, wdn_ref, p_ref, pg_ref, wg_ref,
              wp_ref, fg_ref, o_ref, h_ref, act_ref, *, tiles_per_seq, tf, final):
    tm = x_ref.shape[0]
    dff = wdn_ref.shape[0]
    ext = tm + 2 * CONV_HALO
    i = pl.program_id(0)
    has_prev = (i % tiles_per_seq != 0).astype(F32)
    has_next = (i % tiles_per_seq != tiles_per_seq - 1).astype(F32)
    g = g_ref[...]
    h_ref[0:tm, :] = _rms(x_ref[...], g).astype(BF16)
    halo = jnp.concatenate([_rms(xn_ref[...], g) * has_next, _rms(xp_ref[...], g) * has_prev], axis=0)
    h_ref[tm:ext, :] = halo.astype(BF16)
    h = h_ref[...]

    def conv(u, c0):
        w = cw_ref[:, c0:c0 + tf]
        out = (pltpu.roll(u, 1, 0) * w[0:1] + u * w[1:2] + pltpu.roll(u, ext - 1, 0) * w[2:3]
               + cb_ref[:, c0:c0 + tf])
        return out[0:tm]

    for j in range(dff // tf):
        ua = jnp.dot(h, wup_ref[:, j * tf:(j + 1) * tf], preferred_element_type=F32)
        ug = jnp.dot(h, wup_ref[:, dff + j * tf:dff + (j + 1) * tf], preferred_element_type=F32)
        act_ref[:, j * tf:(j + 1) * tf] = (_gelu(conv(ug, dff + j * tf)) * conv(ua, j * tf)).astype(BF16)
    x = x_ref[...] + jnp.dot(act_ref[...], wdn_ref[...], preferred_element_type=F32)
    gate = _sigmoid(jnp.dot(_rms(x, pg_ref[...]).astype(BF16), wg_ref[...], preferred_element_type=F32))
    out = x + gate * jnp.dot(p_ref[...].astype(BF16), wp_ref[...], preferred_element_type=F32)
    if final:
        out = _rms(out, fg_ref[...])
    o_ref[...] = out


def _ffn_ple(x, seq, g, wup, cw, cb, wdn, p, layer, pg, wg, wp, fg, final, tm=512, tf=256):
    m, d = x.shape
    dff = wdn.shape[0]
    pd = p.shape[2]
    hb = tm // CONV_HALO
    nblk = m // CONV_HALO
    return pl.pallas_call(
        functools.partial(_ffn_body, tiles_per_seq=seq // tm, tf=tf, final=final),
        grid=(m // tm,),
        in_specs=[pl.BlockSpec((tm, d), lambda i: (i, 0)),
                  pl.BlockSpec((CONV_HALO, d), lambda i: (jnp.maximum(i * hb - 1, 0), 0)),
                  pl.BlockSpec((CONV_HALO, d), lambda i: (jnp.minimum((i + 1) * hb, nblk - 1), 0)),
                  _const_spec((1, d)), _const_spec(wup.shape), _const_spec(cw.shape),
                  _const_spec((1, 2 * dff)), _const_spec(wdn.shape),
                  pl.BlockSpec((None, tm, pd), lambda i: (layer, i, 0)),
                  _const_spec((1, d)), _const_spec(wg.shape), _const_spec(wp.shape), _const_spec((1, d))],
        out_specs=pl.BlockSpec((tm, d), lambda i: (i, 0)),
        out_shape=jax.ShapeDtypeStruct((m, d), F32),
        scratch_shapes=[pltpu.VMEM((tm + 2 * CONV_HALO, d), BF16), pltpu.VMEM((tm, dff), BF16)],
        compiler_params=_cparams(("parallel",)),
        name="ffn_ple",
    )(x, x, x, g.reshape(1, d), wup, cw, cb.reshape(1, 2 * dff), wdn, p, pg.reshape(1, d), wg, wp,
      fg.reshape(1, d))


def _ret_in_body(x_ref, g_ref, w_ref, cos_ref, sin_ref, q_ref, k_ref, v_ref, gt_ref):
    h = _rms(x_ref[...], g_ref[...]).astype(BF16)
    cos, sin = cos_ref[...], sin_ref[...]
    half = RET_QK // 2
    dq = q_ref.shape[1]
    dv = v_ref.shape[1]

    def rot(z, scale):
        x1, x2 = z[:, :half], z[:, half:]
        return jnp.concatenate([(x1 * cos - x2 * sin) * scale, (x1 * sin + x2 * cos) * scale], axis=-1)

    for j in range(dq // RET_QK):
        c0 = j * RET_QK
        zq = jnp.dot(h, w_ref[:, c0:c0 + RET_QK], preferred_element_type=F32)
        q_ref[:, c0:c0 + RET_QK] = rot(zq, 1.0).astype(BF16)
        zk = jnp.dot(h, w_ref[:, dq + c0:dq + c0 + RET_QK], preferred_element_type=F32)
        k_ref[:, c0:c0 + RET_QK] = rot(zk, RET_QK ** -0.5).astype(BF16)
    tn = 512
    for j in range(dv // tn):
        c0 = j * tn
        v_ref[:, c0:c0 + tn] = jnp.dot(h, w_ref[:, 2 * dq + c0:2 * dq + c0 + tn],
                                       preferred_element_type=F32).astype(BF16)
        gt_ref[:, c0:c0 + tn] = jnp.dot(h, w_ref[:, 2 * dq + dv + c0:2 * dq + dv + c0 + tn],
                                        preferred_element_type=F32).astype(BF16)


def _ret_in_proj(x, seq, g, w, cos, sin, tm=512):
    m, d = x.shape
    dq = RET_HEADS * RET_QK
    dv = RET_HEADS * RET_V
    tps = seq // tm
    row = lambda n: pl.BlockSpec((tm, n), lambda i: (i, 0))
    pos = pl.BlockSpec((tm, RET_QK // 2), lambda i: (i % tps, 0))
    return pl.pallas_call(
        _ret_in_body,
        grid=(m // tm,),
        in_specs=[row(d), _const_spec((1, d)), _const_spec(w.shape), pos, pos],
        out_specs=[row(dq), row(dq), row(dv), row(dv)],
        out_shape=[jax.ShapeDtypeStruct((m, dq), BF16), jax.ShapeDtypeStruct((m, dq), BF16),
                   jax.ShapeDtypeStruct((m, dv), BF16), jax.ShapeDtypeStruct((m, dv), BF16)],
        compiler_params=_cparams(("parallel",)),
        name="ret_in_proj",
    )(x, g.reshape(1, d), w, cos, sin)


def _ret_tables(decay_param):
    L = RET_CHUNK
    lg = -jnp.exp(decay_param.astype(F32))
    lg_f, lg_b = lg[0][:, None], lg[1][:, None]
    pos = jnp.arange(L, dtype=F32)
    diff = pos[:, None] - pos[None, :]
    dmat = jnp.where(diff >= 0, jnp.exp(lg_f[:, :, None] * jnp.abs(diff)),
                     jnp.exp(lg_b[:, :, None] * jnp.abs(diff)))
    vecs = jnp.stack([jnp.exp(lg_f * (pos + 1.0)),
                      jnp.exp(lg_f * (L - 1.0 - pos)),
                      jnp.exp(lg_b * (L - pos)),
                      jnp.exp(lg_b * pos)], axis=-1)
    cdec = jnp.exp(lg * L).T
    return dmat, vecs, cdec


def _ret_body(cdec_ref, q_ref, k_ref, v_ref, gt_ref, dmat_ref, vec_ref, o_ref, sb_ref, st_ref):
    L = RET_CHUNK
    seq = q_ref.shape[1]
    n = seq // L
    hd = pl.program_id(1)
    cf = cdec_ref[hd, 0]
    cb = cdec_ref[hd, 1]
    vec = vec_ref[0]
    qf, kf, qb, kb = vec[:, 0:1], vec[:, 1:2], vec[:, 2:3], vec[:, 3:4]
    contract0 = (((0,), (0,)), ((), ()))

    def scaled(t, col):
        return (t.astype(F32) * col).astype(BF16)

    st_ref[...] = jnp.zeros_like(st_ref)

    def bwd(i, carry):
        c = n - 1 - i
        rs = pl.ds(pl.multiple_of(c * L, L), L)
        sb_ref[c] = st_ref[...].astype(BF16)
        st_ref[...] = cb * st_ref[...] + lax.dot_general(scaled(k_ref[0, rs, :], kb), v_ref[0, rs, :],
                                                         contract0, preferred_element_type=F32)
        return carry

    lax.fori_loop(0, n, bwd, 0, unroll=2)

    st_ref[...] = jnp.zeros_like(st_ref)

    def fwd(c, carry):
        rs = pl.ds(pl.multiple_of(c * L, L), L)
        q_c, k_c, v_c = q_ref[0, rs, :], k_ref[0, rs, :], v_ref[0, rs, :]
        scores = lax.dot_general(q_c, k_c, (((1,), (1,)), ((), ())), preferred_element_type=F32)
        y = jnp.dot((scores * dmat_ref[0]).astype(BF16), v_c, preferred_element_type=F32)
        y = y + jnp.dot(scaled(q_c, qf), st_ref[...].astype(BF16), preferred_element_type=F32)
        y = y + jnp.dot(scaled(q_c, qb), sb_ref[c], preferred_element_type=F32)
        st_ref[...] = cf * st_ref[...] + lax.dot_general(scaled(k_c, kf), v_c, contract0,
                                                         preferred_element_type=F32)
        y = y * lax.rsqrt(jnp.mean(y * y, axis=-1, keepdims=True) + EPS)
        gt = gt_ref[0, rs, :].astype(F32)
        o_ref[0, rs, :] = (gt * _sigmoid(gt) * y).astype(o_ref.dtype)
        return carry

    lax.fori_loop(0, n, fwd, 0, unroll=2)


def _retention(q, k, v, gt, tables):
    dmat, vecs, cdec = tables
    b, s, _ = q.shape
    L = RET_CHUNK
    qk_spec = pl.BlockSpec((1, s, RET_QK), lambda i, h, *_: (i, 0, h))
    v_spec = pl.BlockSpec((1, s, RET_V), lambda i, h, *_: (i, 0, h))
    grid_spec = pltpu.PrefetchScalarGridSpec(
        num_scalar_prefetch=1,
        grid=(b, RET_HEADS),
        in_specs=[qk_spec, qk_spec, v_spec, v_spec,
                  pl.BlockSpec((1, L, L), lambda i, h, *_: (h, 0, 0)),
                  pl.BlockSpec((1, L, 4), lambda i, h, *_: (h, 0, 0))],
        out_specs=v_spec,
        scratch_shapes=[pltpu.VMEM((s // L, RET_QK, RET_V), BF16), pltpu.VMEM((RET_QK, RET_V), F32)],
    )
    return pl.pallas_call(
        _ret_body,
        grid_spec=grid_spec,
        out_shape=jax.ShapeDtypeStruct((b, s, RET_HEADS * RET_V), BF16),
        compiler_params=_cparams(("parallel", "parallel")),
        name="retention",
    )(cdec, q, k, v, gt, dmat, vecs)


def _proj_residual_body(x_ref, y_ref, w_ref, o_ref):
    o_ref[...] = x_ref[...] + jnp.dot(y_ref[...], w_ref[...], preferred_element_type=F32)


def _proj_residual(x, y, w, tm=512):
    m, d = x.shape
    kd = y.shape[1]
    return pl.pallas_call(
        _proj_residual_body,
        grid=(m // tm,),
        in_specs=[pl.BlockSpec((tm, d), lambda i: (i, 0)), pl.BlockSpec((tm, kd), lambda i: (i, 0)),
                  _const_spec(w.shape)],
        out_specs=pl.BlockSpec((tm, d), lambda i: (i, 0)),
        out_shape=jax.ShapeDtypeStruct((m, d), F32),
        compiler_params=_cparams(("parallel",)),
        name="ret_out_proj",
    )(x, y, w)


def _na_s5_layer(x, bsz, seq, norm, w_in, rpb, lam_re, lam_im, log_dt, b_re, b_im, c_re, c_im, d_skip,
                 w_glu, b_glu, w_out):
    m, d = x.shape
    half = NA_HEADS * NA_HEAD_DIM
    col_scale = jnp.concatenate([jnp.full((half,), NA_HEAD_DIM ** -0.5 * LOG2E, F32),
                                 jnp.ones((3 * half,), F32)])
    z, u_pairs = _ab_in_proj(x.reshape(bsz, seq, d), norm, (w_in * col_scale).astype(BF16), 3 * half)
    a_out = _na_attention(z, _na_bias_table(rpb))
    mats = _s5_matrices(lam_re, lam_im, log_dt, b_re, b_im, c_re, c_im, d_skip)
    pairs, nc = u_pairs.shape[0], u_pairs.shape[1]
    y_pairs = _s5_scan(u_pairs.reshape(pairs, nc * bsz, half), mats, bsz).reshape(pairs, nc, bsz, half)
    out = _ab_out(x.reshape(bsz, seq, d), a_out, y_pairs, w_glu.astype(BF16), b_glu, w_out.astype(BF16))
    return out.reshape(m, d)


def _rope_tables(seq):
    half = RET_QK // 2
    inv_freq = ROPE_BASE ** (-jnp.arange(half, dtype=F32) / half)
    ang = jnp.arange(seq, dtype=F32)[:, None] * inv_freq[None, :]
    return jnp.cos(ang), jnp.sin(ang)


def _retention_layer(x, bsz, seq, norm, w_in, decay, w_out):
    m, d = x.shape
    cos, sin = _rope_tables(seq)
    q, k, v, gt = _ret_in_proj(x, seq, norm, w_in.astype(BF16), cos, sin)
    shp = lambda t: t.reshape(bsz, seq, t.shape[-1])
    y = _retention(shp(q), shp(k), shp(v), shp(gt), _ret_tables(decay))
    return _proj_residual(x, y.reshape(m, -1), w_out.astype(BF16))


def kernel(x, p, ab_norm, ab_w_in, na_rpb, s5_lambda_re, s5_lambda_im, s5_log_dt, s5_b_re, s5_b_im,
           s5_c_re, s5_c_im, s5_d, s5_w_glu, s5_b_glu, ab_w_out, ret_norm, ret_w_in, ret_decay, ret_w_out,
           ffn_norm, ffn_w_up, ffn_conv_w, ffn_conv_b, ffn_w_down, ple_norm, ple_w_gate, ple_w_proj,
           final_norm):
    bsz, seq, d = x.shape
    depth = p.shape[0]
    m = bsz * seq
    xs = x.reshape(m, d)
    p_tok = p.reshape(depth, m, p.shape[-1])
    for i in range(depth):
        j = i // 2
        if i % 2 == 0:
            xs = _na_s5_layer(xs, bsz, seq, ab_norm[j], ab_w_in[j], na_rpb[j], s5_lambda_re[j],
                              s5_lambda_im[j], s5_log_dt[j], s5_b_re[j], s5_b_im[j], s5_c_re[j],
                              s5_c_im[j], s5_d[j], s5_w_glu[j], s5_b_glu[j], ab_w_out[j])
        else:
            xs = _retention_layer(xs, bsz, seq, ret_norm[j], ret_w_in[j], ret_decay[j], ret_w_out[j])
        xs = _ffn_ple(xs, seq, ffn_norm[i], ffn_w_up[i].astype(BF16), ffn_conv_w[i], ffn_conv_b[i],
                      ffn_w_down[i].astype(BF16), p_tok, i, ple_norm[i], ple_w_gate[i].astype(BF16),
                      ple_w_proj[i].astype(BF16), final_norm, final=(i == depth - 1))
    return xs.reshape(bsz, seq, d)
```

```python
import functools

import jax
import jax.numpy as jnp
from jax import lax
from jax.experimental import pallas as pl
from jax.experimental.pallas import tpu as pltpu

F32 = jnp.float32
BF16 = jnp.bfloat16

LANES = 128
EPS = 1e-6
GRID_W = 64
NA_HEADS = 8
NA_HEAD_DIM = 64
NA_WIN_ROWS = 8
NA_WIN_COLS = 16
NA_MASK = -1e30
LOG2E = 1.4426950408889634
NA_HEAD_BLOCK = 4
S5_GROUP = 16
S5_STATE = 64
S5_CHUNK = 16
S5_SLOT = 2 * S5_GROUP
S5_SLOTS = LANES // S5_SLOT
AB_TOKENS = 32
RET_HEADS = 4
RET_QK = 256
RET_V = 512
RET_CHUNK = 256
ROPE_BASE = 10000.0
CONV_HALO = 8
VMEM_LIMIT = 56 * 1024 * 1024


def _cparams(sem):
    return pltpu.CompilerParams(dimension_semantics=sem, vmem_limit_bytes=VMEM_LIMIT)


def _const_spec(shape):
    nd = len(shape)
    return pl.BlockSpec(shape, lambda *_: (0,) * nd, pipeline_mode=pl.Buffered(1))


def _rms(xf, g):
    ms = jnp.mean(xf * xf, axis=-1, keepdims=True)
    return xf * lax.rsqrt(ms + EPS) * g


def _gelu(x):
    return 0.5 * x * (1.0 + lax.erf(x * (2.0 ** -0.5)))


def _sigmoid(x):
    return 0.5 * jnp.tanh(0.5 * x) + 0.5


def _slot_transpose4(a, lane):
    lo = lane < 2 * S5_SLOT
    even = (lane % (2 * S5_SLOT)) < S5_SLOT
    b0 = jnp.where(lo, a[0], pltpu.roll(a[2], 2 * S5_SLOT, 1))
    b2 = jnp.where(lo, pltpu.roll(a[0], 2 * S5_SLOT, 1), a[2])
    b1 = jnp.where(lo, a[1], pltpu.roll(a[3], 2 * S5_SLOT, 1))
    b3 = jnp.where(lo, pltpu.roll(a[1], 2 * S5_SLOT, 1), a[3])
    return [jnp.where(even, b0, pltpu.roll(b1, S5_SLOT, 1)),
            jnp.where(even, pltpu.roll(b0, LANES - S5_SLOT, 1), b1),
            jnp.where(even, b2, pltpu.roll(b3, S5_SLOT, 1)),
            jnp.where(even, pltpu.roll(b2, LANES - S5_SLOT, 1), b3)]


def _ab_in_body(x_ref, g_ref, w_ref, z_ref, u_ref, zs_ref, *, tn):
    bsz, tt, d = x_ref.shape
    h = _rms(x_ref[...].reshape(bsz * tt, d), g_ref[...]).astype(BF16)
    nz = z_ref.shape[2]
    for j in range(nz // tn):
        z_ref[:, :, j * tn:(j + 1) * tn] = jnp.dot(
            h, w_ref[:, j * tn:(j + 1) * tn], preferred_element_type=F32
        ).astype(z_ref.dtype).reshape(bsz, tt, tn)
    zu = jnp.dot(h, w_ref[:, nz:], preferred_element_type=F32)
    for w in range(zs_ref.shape[0]):
        zs_ref[w] = zu[:, w * LANES:(w + 1) * LANES]
    lane = lax.broadcasted_iota(jnp.int32, (bsz, LANES), 1)
    for cl in range(tt // S5_CHUNK):
        for v in range(S5_CHUNK // S5_SLOTS):
            for w in range(zs_ref.shape[0]):
                pos = cl * S5_CHUNK + S5_SLOTS * v
                outs = _slot_transpose4([zs_ref[w, pl.ds(pos + i, bsz, stride=tt), :]
                                         for i in range(S5_SLOTS)], lane)
                for j in range(S5_SLOTS):
                    u_ref[S5_SLOTS * w + j, cl, :, v * LANES:(v + 1) * LANES] = outs[j].astype(u_ref.dtype)


def _ab_in_proj(x, g, w, nz, tn=512):
    bsz, seq, d = x.shape
    nu = w.shape[1] - nz
    pairs = nu // S5_SLOT
    assert S5_CHUNK * S5_SLOT == nu and pairs == (nu // LANES) * S5_SLOTS
    tt = AB_TOKENS
    return pl.pallas_call(
        functools.partial(_ab_in_body, tn=tn),
        grid=(seq // tt,),
        in_specs=[pl.BlockSpec((bsz, tt, d), lambda i: (0, i, 0)), _const_spec((1, d)), _const_spec(w.shape)],
        out_specs=[pl.BlockSpec((bsz, tt, nz), lambda i: (0, i, 0)),
                   pl.BlockSpec((pairs, tt // S5_CHUNK, bsz, nu), lambda i: (0, i, 0, 0))],
        out_shape=[jax.ShapeDtypeStruct((bsz, seq, nz), BF16),
                   jax.ShapeDtypeStruct((pairs, seq // S5_CHUNK, bsz, nu), BF16)],
        scratch_shapes=[pltpu.VMEM((nu // LANES, bsz * tt, LANES), F32)],
        compiler_params=_cparams(("parallel",)),
        name="ab_in_proj",
    )(x, g.reshape(1, d), w)


def _na_bias_table(rpb):
    cols = jnp.arange(GRID_W)
    cs = jnp.clip(cols - NA_WIN_COLS // 2, 0, GRID_W - NA_WIN_COLS)
    j = jnp.arange(GRID_W)
    inwin = (j[None, :] >= cs[:, None]) & (j[None, :] < cs[:, None] + NA_WIN_COLS)
    dc = jnp.clip(j[None, :] - cols[:, None] + NA_WIN_COLS - 1, 0, 2 * NA_WIN_COLS - 2)
    t = (rpb.astype(F32) * LOG2E)[:, :, dc]
    t = jnp.where(inwin[None, None], t, NA_MASK).transpose(0, 2, 1, 3)
    h = rpb.shape[0]
    kw = NA_WIN_ROWS * GRID_W
    return jnp.stack([t[:, :, v:v + NA_WIN_ROWS, :].reshape(h // NA_HEAD_BLOCK, NA_HEAD_BLOCK * GRID_W, kw)
                      for v in range(NA_WIN_ROWS)])


def _na_body(q_ref, k_ref, v_ref, bias_ref, o_ref, *, rows):
    kwin = NA_WIN_ROWS * GRID_W
    bw = NA_HEAD_BLOCK * NA_HEAD_DIM
    head_of_lane = lax.broadcasted_iota(jnp.int32, (GRID_W, bw), 1) // NA_HEAD_DIM

    def row_fn(r, carry):
        rs = jnp.clip(r - NA_WIN_ROWS // 2, 0, rows - NA_WIN_ROWS)
        variant = rs - r + (NA_WIN_ROWS - 1)
        qrow = pl.ds(pl.multiple_of(r * GRID_W, GRID_W), GRID_W)
        krow = pl.ds(pl.multiple_of(rs * GRID_W, GRID_W), kwin)
        for blk in range(NA_HEADS // NA_HEAD_BLOCK):
            sl = slice(blk * bw, (blk + 1) * bw)
            q_p, k_p, v_p = q_ref[0, qrow, sl], k_ref[0, krow, sl], v_ref[0, krow, sl]
            q_all = jnp.concatenate([jnp.where(head_of_lane == hh, q_p, jnp.zeros_like(q_p))
                                     for hh in range(NA_HEAD_BLOCK)], axis=0)
            logits = lax.dot_general(q_all, k_p, (((1,), (1,)), ((), ())), preferred_element_type=F32)
            logits = logits + bias_ref[variant, blk]
            e = jnp.exp2(logits - jnp.max(logits, axis=-1, keepdims=True))
            den = jnp.sum(e, axis=-1, keepdims=True)
            o = jnp.dot(e.astype(BF16), v_p, preferred_element_type=F32) / den
            out = o[0:GRID_W]
            for hh in range(1, NA_HEAD_BLOCK):
                out = jnp.where(head_of_lane == hh, o[hh * GRID_W:(hh + 1) * GRID_W], out)
            o_ref[0, qrow, sl] = out.astype(o_ref.dtype)
        return carry

    lax.fori_loop(0, rows, row_fn, 0, unroll=16)


def _na_attention(z, bias):
    b, s, _ = z.shape
    width = NA_HEADS * NA_HEAD_DIM
    rows = s // GRID_W
    spec = lambda c: pl.BlockSpec((1, s, width), lambda i, c=c: (i, 0, c))
    return pl.pallas_call(
        functools.partial(_na_body, rows=rows),
        grid=(b,),
        in_specs=[spec(0), spec(1), spec(2), _const_spec(bias.shape)],
        out_specs=pl.BlockSpec((1, s, width), lambda i: (i, 0, 0)),
        out_shape=jax.ShapeDtypeStruct((b, s, width), BF16),
        compiler_params=_cparams(("parallel",)),
        name="na_attention",
    )(z, z, z, bias)


def _s5_matrices(lam_re, lam_im, log_dt, b_re, b_im, c_re, c_im, d_skip):
    L, hg, p = S5_CHUNK, S5_GROUP, S5_STATE
    g = lam_re.shape[1]
    gp = g // 2
    tau = jnp.arange(L + 1, dtype=F32)
    eye2 = jnp.eye(2, dtype=F32)
    pw, bbar_t, cc = [], [], []
    for d in range(2):
        lam = lax.complex(lam_re[d].astype(F32), lam_im[d].astype(F32))
        lam_dt = lam * jnp.exp(log_dt[d].astype(F32))[:, None]
        lam_bar = jnp.exp(lam_dt)
        pw.append(jnp.exp(lam_dt[None] * tau[:, None, None]))
        b_c = lax.complex(b_re[d].astype(F32), b_im[d].astype(F32))
        bbar_t.append((((lam_bar - 1.0) / lam)[:, :, None] * b_c).transpose(0, 2, 1))
        cc.append(lax.complex(c_re[d].astype(F32), c_im[d].astype(F32)))

    kf = jnp.real(jnp.einsum('gnp,tgp,ghp->tghn', cc[0], pw[0][:L], bbar_t[0]))
    kb = jnp.real(jnp.einsum('gnp,tgp,ghp->tghn', cc[1], pw[1][:L], bbar_t[1]))
    skip = jnp.eye(hg, dtype=F32)[None] * d_skip.astype(F32).reshape(g, 1, hg)
    tab = jnp.concatenate([kb[1:][::-1], (kf[0] + kb[0] + skip)[None], kf[1:]])
    tab = jnp.einsum('aqihn,ij->qihajn', tab.reshape(2 * L - 1, gp, 2, hg, hg), eye2)
    tab = tab.reshape(gp, 2 * hg, (2 * L - 1) * 2 * hg)
    lag_tab = jnp.pad(tab, ((0, 0), (0, 0), (0, 2 * hg)))

    a_f = pw[0][L - 1 - jnp.arange(L)][:, :, None, :] * bbar_t[0][None]
    a_b = pw[1][jnp.arange(L)][:, :, None, :] * bbar_t[1][None]
    loc = jnp.stack([jnp.real(a_f), jnp.imag(a_f), jnp.real(a_b), jnp.imag(a_b)])

    e_f = cc[0][None] * pw[0][1 + jnp.arange(L)][:, :, None, :]
    e_b = cc[1][None] * pw[1][L - jnp.arange(L)][:, :, None, :]
    cr = jnp.stack([jnp.real(e_f), -jnp.imag(e_f), jnp.real(e_b), -jnp.imag(e_b)])

    lam_l = jnp.stack([jnp.real(pw[0][L]), jnp.imag(pw[0][L]),
                       jnp.real(pw[1][L]), jnp.imag(pw[1][L])])
    lam_l = lam_l.reshape(4, gp, 2 * p).transpose(1, 0, 2)
    return lag_tab, loc.reshape(4, L, gp, 2 * hg, p), cr.reshape(4, L, gp, 2 * hg, p), lam_l


def _s5_body(u_ref, tab_ref, loc_ref, cross_ref, lam_ref, y_ref, mintra_ref, mloc_ref, mcross_ref,
             xloc_ref, xin_ref, *, bsz, tr):
    rows, width = u_ref.shape[1], u_ref.shape[2]
    nc = rows // bsz
    w = lam_ref.shape[2]

    tab = tab_ref[0]
    for s in range(S5_CHUNK):
        off = (S5_CHUNK - 1 - s) * S5_SLOT
        mintra_ref[s * S5_SLOT:(s + 1) * S5_SLOT, :] = tab[:, off:off + width].astype(BF16)
    own = (lax.broadcasted_iota(jnp.int32, (S5_SLOT, w), 0) // S5_GROUP
           == lax.broadcasted_iota(jnp.int32, (S5_SLOT, w), 1) // S5_STATE)
    for part in range(4):
        for s in range(S5_CHUNK):
            dst = (slice(s * S5_SLOT, (s + 1) * S5_SLOT), slice(part * w, (part + 1) * w))
            for tab_ref_, m_ref in ((loc_ref, mloc_ref), (cross_ref, mcross_ref)):
                blk = tab_ref_[part, s]
                m_ref[dst] = jnp.where(own, jnp.concatenate([blk, blk], axis=1), 0.0).astype(BF16)

    def loc_fn(i, carry):
        rs = pl.ds(pl.multiple_of(i * tr, tr), tr)
        xloc = jnp.dot(u_ref[0, rs, :], mloc_ref[...], preferred_element_type=F32)
        for part in range(4):
            xloc_ref[part, rs, :] = xloc[:, part * w:(part + 1) * w]
        return carry

    lax.fori_loop(0, rows // tr, loc_fn, 0)

    afr, afi, abr, abi = lam_ref[0, 0:1, :], lam_ref[0, 1:2, :], lam_ref[0, 2:3, :], lam_ref[0, 3:4, :]

    def carry_fn(i, state):
        sfr, sfi, sbr, sbi = state
        rf = pl.ds(pl.multiple_of(i * bsz, bsz), bsz)
        rb = pl.ds(pl.multiple_of((nc - 1 - i) * bsz, bsz), bsz)
        xin_ref[0, rf, :] = sfr
        xin_ref[1, rf, :] = sfi
        xin_ref[2, rb, :] = sbr
        xin_ref[3, rb, :] = sbi
        nfr = afr * sfr - afi * sfi + xloc_ref[0, rf, :]
        nfi = afr * sfi + afi * sfr + xloc_ref[1, rf, :]
        nbr = abr * sbr - abi * sbi + xloc_ref[2, rb, :]
        nbi = abr * sbi + abi * sbr + xloc_ref[3, rb, :]
        return nfr, nfi, nbr, nbi

    zero = jnp.zeros((bsz, w), F32)
    lax.fori_loop(0, nc, carry_fn, (zero, zero, zero, zero), unroll=2)

    def out_fn(i, carry):
        rs = pl.ds(pl.multiple_of(i * tr, tr), tr)
        y = jnp.dot(u_ref[0, rs, :], mintra_ref[...], preferred_element_type=F32)
        xin = jnp.concatenate([xin_ref[part, rs, :] for part in range(4)], axis=1)
        y = y + lax.dot_general(xin.astype(BF16), mcross_ref[...], (((1,), (1,)), ((), ())),
                                preferred_element_type=F32)
        y_ref[0, rs, :] = y
        return carry

    lax.fori_loop(0, rows // tr, out_fn, 0)


def _s5_scan(u_pairs, mats, bsz):
    lag_tab, loc_tab, cross_tab, lam_l = mats
    gp, rows, width = u_pairs.shape
    tr = min(512, rows)
    w = lam_l.shape[2]
    pair_spec = lambda a: pl.BlockSpec((1,) + a.shape[1:], lambda i: (i, 0, 0))
    state_spec = pl.BlockSpec(loc_tab.shape[:2] + (None,) + loc_tab.shape[3:], lambda i: (0, 0, i, 0, 0))
    return pl.pallas_call(
        functools.partial(_s5_body, bsz=bsz, tr=tr),
        grid=(gp,),
        in_specs=[pair_spec(u_pairs), pair_spec(lag_tab), state_spec, state_spec, pair_spec(lam_l)],
        out_specs=pl.BlockSpec((1, rows, width), lambda i: (i, 0, 0)),
        out_shape=jax.ShapeDtypeStruct((gp, rows, width), F32),
        scratch_shapes=[pltpu.VMEM((width, width), BF16), pltpu.VMEM((width, 4 * w), BF16),
                        pltpu.VMEM((width, 4 * w), BF16),
                        pltpu.VMEM((4, rows, w), F32), pltpu.VMEM((4, rows, w), F32)],
        compiler_params=_cparams(("parallel",)),
        name="s5_scan",
    )(u_pairs, lag_tab, loc_tab, cross_tab, lam_l)


def _ab_out_body(x_ref, a_ref, y_ref, wglu_ref, bglu_ref, wout_ref, o_ref, ys_ref):
    bsz, tt, d = x_ref.shape
    half = a_ref.shape[2]
    lane = lax.broadcasted_iota(jnp.int32, (bsz, LANES), 1)
    for cl in range(tt // S5_CHUNK):
        for v in range(S5_CHUNK // S5_SLOTS):
            for w in range(half // LANES):
                outs = _slot_transpose4([y_ref[S5_SLOTS * w + j, cl, :, v * LANES:(v + 1) * LANES]
                                         for j in range(S5_SLOTS)], lane)
                pos = cl * S5_CHUNK + S5_SLOTS * v
                for i in range(S5_SLOTS):
                    ys_ref[w, pl.ds(pos + i, bsz, stride=tt), :] = outs[i]
    yg = _gelu(jnp.concatenate([ys_ref[w] for w in range(half // LANES)], axis=1))
    gate = _sigmoid(jnp.dot(yg.astype(BF16), wglu_ref[...], preferred_element_type=F32) + bglu_ref[...])
    b_out = (yg * gate).astype(BF16)
    a = a_ref[...].reshape(bsz * tt, half)
    acc = x_ref[...].reshape(bsz * tt, d) + jnp.dot(a, wout_ref[0:half, :], preferred_element_type=F32)
    acc = acc + jnp.dot(b_out, wout_ref[half:, :], preferred_element_type=F32)
    o_ref[...] = acc.reshape(bsz, tt, d)


def _ab_out(x, a, y_pairs, wglu, bglu, wout):
    bsz, seq, d = x.shape
    half = a.shape[2]
    pairs = y_pairs.shape[0]
    tt = AB_TOKENS
    tok = lambda n: pl.BlockSpec((bsz, tt, n), lambda i: (0, i, 0))
    return pl.pallas_call(
        _ab_out_body,
        grid=(seq // tt,),
        in_specs=[tok(d), tok(half),
                  pl.BlockSpec((pairs, tt // S5_CHUNK, bsz, half), lambda i: (0, i, 0, 0)),
                  _const_spec(wglu.shape), _const_spec((1, half)), _const_spec(wout.shape)],
        out_specs=tok(d),
        out_shape=jax.ShapeDtypeStruct((bsz, seq, d), F32),
        scratch_shapes=[pltpu.VMEM((half // LANES, bsz * tt, LANES), F32)],
        compiler_params=_cparams(("parallel",)),
        name="ab_out_proj",
    )(x, a, y_pairs, wglu, bglu.reshape(1, half), wout)


def _ffn_body(x_ref, xp_ref, xn_ref, g_ref, wup_ref, cw_ref, cb_ref, wdn_ref, p_ref, pg_ref, wg_ref,
              wp_ref, fg_ref, o_ref, h_ref, act_ref, *, tiles_per_seq, tf, final):
    tm = x_ref.shape[0]
    dff = wdn_ref.shape[0]
    ext = tm + 2 * CONV_HALO
    i = pl.program_id(0)
    has_prev = (i % tiles_per_seq != 0).astype(F32)
    has_next = (i % tiles_per_seq != tiles_per_seq - 1).astype(F32)
    g = g_ref[...]
    h_ref[0:tm, :] = _rms(x_ref[...], g).astype(BF16)
    halo = jnp.concatenate([_rms(xn_ref[...], g) * has_next, _rms(xp_ref[...], g) * has_prev], axis=0)
    h_ref[tm:ext, :] = halo.astype(BF16)
    h = h_ref[...]

    def conv(u, c0):
        w = cw_ref[:, c0:c0 + tf]
        out = (pltpu.roll(u, 1, 0) * w[0:1] + u * w[1:2] + pltpu.roll(u, ext - 1, 0) * w[2:3]
               + cb_ref[:, c0:c0 + tf])
        return out[0:tm]

    for j in range(dff // tf):
        ua = jnp.dot(h, wup_ref[:, j * tf:(j + 1) * tf], preferred_element_type=F32)
        ug = jnp.dot(h, wup_ref[:, dff + j * tf:dff + (j + 1) * tf], preferred_element_type=F32)
        act_ref[:, j * tf:(j + 1) * tf] = (_gelu(conv(ug, dff + j * tf)) * conv(ua, j * tf)).astype(BF16)
    x = x_ref[...] + jnp.dot(act_ref[...], wdn_ref[...], preferred_element_type=F32)
    gate = _sigmoid(jnp.dot(_rms(x, pg_ref[...]).astype(BF16), wg_ref[...], preferred_element_type=F32))
    out = x + gate * jnp.dot(p_ref[...].astype(BF16), wp_ref[...], preferred_element_type=F32)
    if final:
        out = _rms(out, fg_ref[...])
    o_ref[...] = out


def _ffn_ple(x, seq, g, wup, cw, cb, wdn, p, layer, pg, wg, wp, fg, final, tm=512, tf=256):
    m, d = x.shape
    dff = wdn.shape[0]
    pd = p.shape[2]
    hb = tm // CONV_HALO
    nblk = m // CONV_HALO
    return pl.pallas_call(
        functools.partial(_ffn_body, tiles_per_seq=seq // tm, tf=tf, final=final),
        grid=(m // tm,),
        in_specs=[pl.BlockSpec((tm, d), lambda i: (i, 0)),
                  pl.BlockSpec((CONV_HALO, d), lambda i: (jnp.maximum(i * hb - 1, 0), 0)),
                  pl.BlockSpec((CONV_HALO, d), lambda i: (jnp.minimum((i + 1) * hb, nblk - 1), 0)),
                  _const_spec((1, d)), _const_spec(wup.shape), _const_spec(cw.shape),
                  _const_spec((1, 2 * dff)), _const_spec(wdn.shape),
                  pl.BlockSpec((None, tm, pd), lambda i: (layer, i, 0)),
                  _const_spec((1, d)), _const_spec(wg.shape), _const_spec(wp.shape), _const_spec((1, d))],
        out_specs=pl.BlockSpec((tm, d), lambda i: (i, 0)),
        out_shape=jax.ShapeDtypeStruct((m, d), F32),
        scratch_shapes=[pltpu.VMEM((tm + 2 * CONV_HALO, d), BF16), pltpu.VMEM((tm, dff), BF16)],
        compiler_params=_cparams(("parallel",)),
        name="ffn_ple",
    )(x, x, x, g.reshape(1, d), wup, cw, cb.reshape(1, 2 * dff), wdn, p, pg.reshape(1, d), wg, wp,
      fg.reshape(1, d))


def _ret_in_body(x_ref, g_ref, w_ref, cos_ref, sin_ref, q_ref, k_ref, v_ref, gt_ref):
    h = _rms(x_ref[...], g_ref[...]).astype(BF16)
    cos, sin = cos_ref[...], sin_ref[...]
    half = RET_QK // 2
    dq = q_ref.shape[1]
    dv = v_ref.shape[1]

    def rot(z, scale):
        x1, x2 = z[:, :half], z[:, half:]
        return jnp.concatenate([(x1 * cos - x2 * sin) * scale, (x1 * sin + x2 * cos) * scale], axis=-1)

    for j in range(dq // RET_QK):
        c0 = j * RET_QK
        zq = jnp.dot(h, w_ref[:, c0:c0 + RET_QK], preferred_element_type=F32)
        q_ref[:, c0:c0 + RET_QK] = rot(zq, 1.0).astype(BF16)
        zk = jnp.dot(h, w_ref[:, dq + c0:dq + c0 + RET_QK], preferred_element_type=F32)
        k_ref[:, c0:c0 + RET_QK] = rot(zk, RET_QK ** -0.5).astype(BF16)
    tn = 512
    for j in range(dv // tn):
        c0 = j * tn
        v_ref[:, c0:c0 + tn] = jnp.dot(h, w_ref[:, 2 * dq + c0:2 * dq + c0 + tn],
                                       preferred_element_type=F32).astype(BF16)
        gt_ref[:, c0:c0 + tn] = jnp.dot(h, w_ref[:, 2 * dq + dv + c0:2 * dq + dv + c0 + tn],
                                        preferred_element_type=F32).astype(BF16)


def _ret_in_proj(x, seq, g, w, cos, sin, tm=512):
    m, d = x.shape
    dq = RET_HEADS * RET_QK
    dv = RET_HEADS * RET_V
    tps = seq // tm
    row = lambda n: pl.BlockSpec((tm, n), lambda i: (i, 0))
    pos = pl.BlockSpec((tm, RET_QK // 2), lambda i: (i % tps, 0))
    return pl.pallas_call(
        _ret_in_body,
        grid=(m // tm,),
        in_specs=[row(d), _const_spec((1, d)), _const_spec(w.shape), pos, pos],
        out_specs=[row(dq), row(dq), row(dv), row(dv)],
        out_shape=[jax.ShapeDtypeStruct((m, dq), BF16), jax.ShapeDtypeStruct((m, dq), BF16),
                   jax.ShapeDtypeStruct((m, dv), BF16), jax.ShapeDtypeStruct((m, dv), BF16)],
        compiler_params=_cparams(("parallel",)),
        name="ret_in_proj",
    )(x, g.reshape(1, d), w, cos, sin)


def _ret_tables(decay_param):
    L = RET_CHUNK
    lg = -jnp.exp(decay_param.astype(F32))
    lg_f, lg_b = lg[0][:, None], lg[1][:, None]
    pos = jnp.arange(L, dtype=F32)
    diff = pos[:, None] - pos[None, :]
    dmat = jnp.where(diff >= 0, jnp.exp(lg_f[:, :, None] * jnp.abs(diff)),
                     jnp.exp(lg_b[:, :, None] * jnp.abs(diff)))
    vecs = jnp.stack([jnp.exp(lg_f * (pos + 1.0)),
                      jnp.exp(lg_f * (L - 1.0 - pos)),
                      jnp.exp(lg_b * (L - pos)),
                      jnp.exp(lg_b * pos)], axis=-1)
    cdec = jnp.exp(lg * L).T
    return dmat, vecs, cdec


def _ret_body(cdec_ref, q_ref, k_ref, v_ref, gt_ref, dmat_ref, vec_ref, o_ref, sb_ref, st_ref):
    L = RET_CHUNK
    seq = q_ref.shape[1]
    n = seq // L
    hd = pl.program_id(1)
    cf = cdec_ref[hd, 0]
    cb = cdec_ref[hd, 1]
    vec = vec_ref[0]
    qf, kf, qb, kb = vec[:, 0:1], vec[:, 1:2], vec[:, 2:3], vec[:, 3:4]
    contract0 = (((0,), (0,)), ((), ()))

    def scaled(t, col):
        return (t.astype(F32) * col).astype(BF16)

    st_ref[...] = jnp.zeros_like(st_ref)

    def bwd(i, carry):
        c = n - 1 - i
        rs = pl.ds(pl.multiple_of(c * L, L), L)
        sb_ref[c] = st_ref[...].astype(BF16)
        st_ref[...] = cb * st_ref[...] + lax.dot_general(scaled(k_ref[0, rs, :], kb), v_ref[0, rs, :],
                                                         contract0, preferred_element_type=F32)
        return carry

    lax.fori_loop(0, n, bwd, 0, unroll=8)

    st_ref[...] = jnp.zeros_like(st_ref)

    def fwd(c, carry):
        rs = pl.ds(pl.multiple_of(c * L, L), L)
        q_c, k_c, v_c = q_ref[0, rs, :], k_ref[0, rs, :], v_ref[0, rs, :]
        scores = lax.dot_general(q_c, k_c, (((1,), (1,)), ((), ())), preferred_element_type=F32)
        y = jnp.dot((scores * dmat_ref[0]).astype(BF16), v_c, preferred_element_type=F32)
        y = y + jnp.dot(scaled(q_c, qf), st_ref[...].astype(BF16), preferred_element_type=F32)
        y = y + jnp.dot(scaled(q_c, qb), sb_ref[c], preferred_element_type=F32)
        st_ref[...] = cf * st_ref[...] + lax.dot_general(scaled(k_c, kf), v_c, contract0,
                                                         preferred_element_type=F32)
        y = y * lax.rsqrt(jnp.mean(y * y, axis=-1, keepdims=True) + EPS)
        gt = gt_ref[0, rs, :].astype(F32)
        o_ref[0, rs, :] = (gt * _sigmoid(gt) * y).astype(o_ref.dtype)
        return carry

    lax.fori_loop(0, n, fwd, 0, unroll=8)


def _retention(q, k, v, gt, tables):
    dmat, vecs, cdec = tables
    b, s, _ = q.shape
    L = RET_CHUNK
    qk_spec = pl.BlockSpec((1, s, RET_QK), lambda i, h, *_: (i, 0, h))
    v_spec = pl.BlockSpec((1, s, RET_V), lambda i, h, *_: (i, 0, h))
    grid_spec = pltpu.PrefetchScalarGridSpec(
        num_scalar_prefetch=1,
        grid=(b, RET_HEADS),
        in_specs=[qk_spec, qk_spec, v_spec, v_spec,
                  pl.BlockSpec((1, L, L), lambda i, h, *_: (h, 0, 0)),
                  pl.BlockSpec((1, L, 4), lambda i, h, *_: (h, 0, 0))],
        out_specs=v_spec,
        scratch_shapes=[pltpu.VMEM((s // L, RET_QK, RET_V), BF16), pltpu.VMEM((RET_QK, RET_V), F32)],
    )
    return pl.pallas_call(
        _ret_body,
        grid_spec=grid_spec,
        out_shape=jax.ShapeDtypeStruct((b, s, RET_HEADS * RET_V), BF16),
        compiler_params=_cparams(("parallel", "parallel")),
        name="retention",
    )(cdec, q, k, v, gt, dmat, vecs)


def _proj_residual_body(x_ref, y_ref, w_ref, o_ref):
    o_ref[...] = x_ref[...] + jnp.dot(y_ref[...], w_ref[...], preferred_element_type=F32)


def _proj_residual(x, y, w, tm=512):
    m, d = x.shape
    kd = y.shape[1]
    return pl.pallas_call(
        _proj_residual_body,
        grid=(m // tm,),
        in_specs=[pl.BlockSpec((tm, d), lambda i: (i, 0)), pl.BlockSpec((tm, kd), lambda i: (i, 0)),
                  _const_spec(w.shape)],
        out_specs=pl.BlockSpec((tm, d), lambda i: (i, 0)),
        out_shape=jax.ShapeDtypeStruct((m, d), F32),
        compiler_params=_cparams(("parallel",)),
        name="ret_out_proj",
    )(x, y, w)


def _na_s5_layer(x, bsz, seq, norm, w_in, rpb, lam_re, lam_im, log_dt, b_re, b_im, c_re, c_im, d_skip,
                 w_glu, b_glu, w_out):
    m, d = x.shape
    half = NA_HEADS * NA_HEAD_DIM
    col_scale = jnp.concatenate([jnp.full((half,), NA_HEAD_DIM ** -0.5 * LOG2E, F32),
                                 jnp.ones((3 * half,), F32)])
    z, u_pairs = _ab_in_proj(x.reshape(bsz, seq, d), norm, (w_in * col_scale).astype(BF16), 3 * half)
    a_out = _na_attention(z, _na_bias_table(rpb))
    mats = _s5_matrices(lam_re, lam_im, log_dt, b_re, b_im, c_re, c_im, d_skip)
    pairs, nc = u_pairs.shape[0], u_pairs.shape[1]
    y_pairs = _s5_scan(u_pairs.reshape(pairs, nc * bsz, half), mats, bsz).reshape(pairs, nc, bsz, half)
    out = _ab_out(x.reshape(bsz, seq, d), a_out, y_pairs, w_glu.astype(BF16), b_glu, w_out.astype(BF16))
    return out.reshape(m, d)


def _rope_tables(seq):
    half = RET_QK // 2
    inv_freq = ROPE_BASE ** (-jnp.arange(half, dtype=F32) / half)
    ang = jnp.arange(seq, dtype=F32)[:, None] * inv_freq[None, :]
    return jnp.cos(ang), jnp.sin(ang)


def _retention_layer(x, bsz, seq, norm, w_in, decay, w_out):
    m, d = x.shape
    cos, sin = _rope_tables(seq)
    q, k, v, gt = _ret_in_proj(x, seq, norm, w_in.astype(BF16), cos, sin)
    shp = lambda t: t.reshape(bsz, seq, t.shape[-1])
    y = _retention(shp(q), shp(k), shp(v), shp(gt), _ret_tables(decay))
    return _proj_residual(x, y.reshape(m, -1), w_out.astype(BF16))


def kernel(x, p, ab_norm, ab_w_in, na_rpb, s5_lambda_re, s5_lambda_im, s5_log_dt, s5_b_re, s5_b_im,
           s5_c_re, s5_c_im, s5_d, s5_w_glu, s5_b_glu, ab_w_out, ret_norm, ret_w_in, ret_decay, ret_w_out,
           ffn_norm, ffn_w_up, ffn_conv_w, ffn_conv_b, ffn_w_down, ple_norm, ple_w_gate, ple_w_proj,
           final_norm):
    bsz, seq, d = x.shape
    depth = p.shape[0]
    m = bsz * seq
    xs = x.reshape(m, d)
    p_tok = p.reshape(depth, m, p.shape[-1])
    for i in range(depth):
        j = i // 2
        if i % 2 == 0:
            xs = _na_s5_layer(xs, bsz, seq, ab_norm[j], ab_w_in[j], na_rpb[j], s5_lambda_re[j],
                              s5_lambda_im[j], s5_log_dt[j], s5_b_re[j], s5_b_im[j], s5_c_re[j],
                              s5_c_im[j], s5_d[j], s5_w_glu[j], s5_b_glu[j], ab_w_out[j])
        else:
            xs = _retention_layer(xs, bsz, seq, ret_norm[j], ret_w_in[j], ret_decay[j], ret_w_out[j])
        xs = _ffn_ple(xs, seq, ffn_norm[i], ffn_w_up[i].astype(BF16), ffn_conv_w[i], ffn_conv_b[i],
                      ffn_w_down[i].astype(BF16), p_tok, i, ple_norm[i], ple_w_gate[i].astype(BF16),
                      ple_w_proj[i].astype(BF16), final_norm, final=(i == depth - 1))
    return xs.reshape(bsz, seq, d)
```

```python
import functools

import jax
import jax.numpy as jnp
from jax import lax
from jax.experimental import pallas as pl
from jax.experimental.pallas import tpu as pltpu

F32 = jnp.float32
BF16 = jnp.bfloat16

LANES = 128
EPS = 1e-6
GRID_W = 64
NA_HEADS = 8
NA_HEAD_DIM = 64
NA_WIN_ROWS = 8
NA_WIN_COLS = 16
NA_MASK = -1e30
LOG2E = 1.4426950408889634
NA_HEAD_BLOCK = 4
S5_GROUP = 16
S5_STATE = 64
S5_CHUNK = 16
S5_SLOT = 2 * S5_GROUP
S5_SLOTS = LANES // S5_SLOT
AB_TOKENS = 32
RET_HEADS = 4
RET_QK = 256
RET_V = 512
RET_CHUNK = 256
ROPE_BASE = 10000.0
CONV_HALO = 8
VMEM_LIMIT = 56 * 1024 * 1024


def _cparams(sem):
    return pltpu.CompilerParams(dimension_semantics=sem, vmem_limit_bytes=VMEM_LIMIT)


def _const_spec(shape):
    nd = len(shape)
    return pl.BlockSpec(shape, lambda *_: (0,) * nd, pipeline_mode=pl.Buffered(1))


def _rms(xf, g):
    ms = jnp.mean(xf * xf, axis=-1, keepdims=True)
    return xf * lax.rsqrt(ms + EPS) * g


def _gelu(x):
    return 0.5 * x * (1.0 + lax.erf(x * (2.0 ** -0.5)))


def _sigmoid(x):
    return 0.5 * jnp.tanh(0.5 * x) + 0.5


def _slot_transpose4(a, lane):
    lo = lane < 2 * S5_SLOT
    even = (lane % (2 * S5_SLOT)) < S5_SLOT
    b0 = jnp.where(lo, a[0], pltpu.roll(a[2], 2 * S5_SLOT, 1))
    b2 = jnp.where(lo, pltpu.roll(a[0], 2 * S5_SLOT, 1), a[2])
    b1 = jnp.where(lo, a[1], pltpu.roll(a[3], 2 * S5_SLOT, 1))
    b3 = jnp.where(lo, pltpu.roll(a[1], 2 * S5_SLOT, 1), a[3])
    return [jnp.where(even, b0, pltpu.roll(b1, S5_SLOT, 1)),
            jnp.where(even, pltpu.roll(b0, LANES - S5_SLOT, 1), b1),
            jnp.where(even, b2, pltpu.roll(b3, S5_SLOT, 1)),
            jnp.where(even, pltpu.roll(b2, LANES - S5_SLOT, 1), b3)]


def _ab_in_body(x_ref, g_ref, w_ref, z_ref, u_ref, zs_ref, *, tn):
    bsz, tt, d = x_ref.shape
    h = _rms(x_ref[...].reshape(bsz * tt, d), g_ref[...]).astype(BF16)
    nz = z_ref.shape[2]
    zu = jnp.dot(h, w_ref[:, nz:], preferred_element_type=F32)
    for w in range(zs_ref.shape[0]):
        zs_ref[w] = zu[:, w * LANES:(w + 1) * LANES]
    lane = lax.broadcasted_iota(jnp.int32, (bsz, LANES), 1)
    for cl in range(tt // S5_CHUNK):
        for v in range(S5_CHUNK // S5_SLOTS):
            for w in range(zs_ref.shape[0]):
                pos = cl * S5_CHUNK + S5_SLOTS * v
                outs = _slot_transpose4([zs_ref[w, pl.ds(pos + i, bsz, stride=tt), :]
                                         for i in range(S5_SLOTS)], lane)
                for j in range(S5_SLOTS):
                    u_ref[S5_SLOTS * w + j, cl, :, v * LANES:(v + 1) * LANES] = outs[j].astype(u_ref.dtype)
    for j in range(nz // tn):
        z_ref[:, :, j * tn:(j + 1) * tn] = jnp.dot(
            h, w_ref[:, j * tn:(j + 1) * tn], preferred_element_type=F32
        ).astype(z_ref.dtype).reshape(bsz, tt, tn)


def _ab_in_proj(x, g, w, nz, tn=512):
    bsz, seq, d = x.shape
    nu = w.shape[1] - nz
    pairs = nu // S5_SLOT
    assert S5_CHUNK * S5_SLOT == nu and pairs == (nu // LANES) * S5_SLOTS
    tt = AB_TOKENS
    return pl.pallas_call(
        functools.partial(_ab_in_body, tn=tn),
        grid=(seq // tt,),
        in_specs=[pl.BlockSpec((bsz, tt, d), lambda i: (0, i, 0)), _const_spec((1, d)), _const_spec(w.shape)],
        out_specs=[pl.BlockSpec((bsz, tt, nz), lambda i: (0, i, 0)),
                   pl.BlockSpec((pairs, tt // S5_CHUNK, bsz, nu), lambda i: (0, i, 0, 0))],
        out_shape=[jax.ShapeDtypeStruct((bsz, seq, nz), BF16),
                   jax.ShapeDtypeStruct((pairs, seq // S5_CHUNK, bsz, nu), BF16)],
        scratch_shapes=[pltpu.VMEM((nu // LANES, bsz * tt, LANES), F32)],
        compiler_params=_cparams(("parallel",)),
        name="ab_in_proj",
    )(x, g.reshape(1, d), w)


def _na_bias_table(rpb):
    cols = jnp.arange(GRID_W)
    cs = jnp.clip(cols - NA_WIN_COLS // 2, 0, GRID_W - NA_WIN_COLS)
    j = jnp.arange(GRID_W)
    inwin = (j[None, :] >= cs[:, None]) & (j[None, :] < cs[:, None] + NA_WIN_COLS)
    dc = jnp.clip(j[None, :] - cols[:, None] + NA_WIN_COLS - 1, 0, 2 * NA_WIN_COLS - 2)
    t = (rpb.astype(F32) * LOG2E)[:, :, dc]
    t = jnp.where(inwin[None, None], t, NA_MASK).transpose(0, 2, 1, 3)
    h = rpb.shape[0]
    kw = NA_WIN_ROWS * GRID_W
    return jnp.stack([t[:, :, v:v + NA_WIN_ROWS, :].reshape(h // NA_HEAD_BLOCK, NA_HEAD_BLOCK * GRID_W, kw)
                      for v in range(NA_WIN_ROWS)])


def _na_body(q_ref, k_ref, v_ref, bias_ref, o_ref, *, rows):
    kwin = NA_WIN_ROWS * GRID_W
    bw = NA_HEAD_BLOCK * NA_HEAD_DIM
    head_of_lane = lax.broadcasted_iota(jnp.int32, (GRID_W, bw), 1) // NA_HEAD_DIM

    def row_fn(r, carry):
        rs = jnp.clip(r - NA_WIN_ROWS // 2, 0, rows - NA_WIN_ROWS)
        variant = rs - r + (NA_WIN_ROWS - 1)
        qrow = pl.ds(pl.multiple_of(r * GRID_W, GRID_W), GRID_W)
        krow = pl.ds(pl.multiple_of(rs * GRID_W, GRID_W), kwin)
        for blk in range(NA_HEADS // NA_HEAD_BLOCK):
            sl = slice(blk * bw, (blk + 1) * bw)
            q_p, k_p, v_p = q_ref[0, qrow, sl], k_ref[0, krow, sl], v_ref[0, krow, sl]
            q_all = jnp.concatenate([jnp.where(head_of_lane == hh, q_p, jnp.zeros_like(q_p))
                                     for hh in range(NA_HEAD_BLOCK)], axis=0)
            logits = lax.dot_general(q_all, k_p, (((1,), (1,)), ((), ())), preferred_element_type=F32)
            logits = logits + bias_ref[variant, blk]
            e = jnp.exp2(logits - jnp.max(logits, axis=-1, keepdims=True))
            den = jnp.sum(e, axis=-1, keepdims=True)
            o = jnp.dot(e.astype(BF16), v_p, preferred_element_type=F32) / den
            out = o[0:GRID_W]
            for hh in range(1, NA_HEAD_BLOCK):
                out = jnp.where(head_of_lane == hh, o[hh * GRID_W:(hh + 1) * GRID_W], out)
            o_ref[0, qrow, sl] = out.astype(o_ref.dtype)
        return carry

    lax.fori_loop(0, rows, row_fn, 0, unroll=16)


def _na_attention(z, bias):
    b, s, _ = z.shape
    width = NA_HEADS * NA_HEAD_DIM
    rows = s // GRID_W
    spec = lambda c: pl.BlockSpec((1, s, width), lambda i, c=c: (i, 0, c))
    return pl.pallas_call(
        functools.partial(_na_body, rows=rows),
        grid=(b,),
        in_specs=[spec(0), spec(1), spec(2), _const_spec(bias.shape)],
        out_specs=pl.BlockSpec((1, s, width), lambda i: (i, 0, 0)),
        out_shape=jax.ShapeDtypeStruct((b, s, width), BF16),
        compiler_params=_cparams(("parallel",)),
        name="na_attention",
    )(z, z, z, bias)


def _s5_matrices(lam_re, lam_im, log_dt, b_re, b_im, c_re, c_im, d_skip):
    L, hg, p = S5_CHUNK, S5_GROUP, S5_STATE
    g = lam_re.shape[1]
    gp = g // 2
    tau = jnp.arange(L + 1, dtype=F32)
    eye2 = jnp.eye(2, dtype=F32)
    pw, bbar_t, cc = [], [], []
    for d in range(2):
        lam = lax.complex(lam_re[d].astype(F32), lam_im[d].astype(F32))
        lam_dt = lam * jnp.exp(log_dt[d].astype(F32))[:, None]
        lam_bar = jnp.exp(lam_dt)
        pw.append(jnp.exp(lam_dt[None] * tau[:, None, None]))
        b_c = lax.complex(b_re[d].astype(F32), b_im[d].astype(F32))
        bbar_t.append((((lam_bar - 1.0) / lam)[:, :, None] * b_c).transpose(0, 2, 1))
        cc.append(lax.complex(c_re[d].astype(F32), c_im[d].astype(F32)))

    kf = jnp.real(jnp.einsum('gnp,tgp,ghp->tghn', cc[0], pw[0][:L], bbar_t[0]))
    kb = jnp.real(jnp.einsum('gnp,tgp,ghp->tghn', cc[1], pw[1][:L], bbar_t[1]))
    skip = jnp.eye(hg, dtype=F32)[None] * d_skip.astype(F32).reshape(g, 1, hg)
    tab = jnp.concatenate([kb[1:][::-1], (kf[0] + kb[0] + skip)[None], kf[1:]])
    tab = jnp.einsum('aqihn,ij->qihajn', tab.reshape(2 * L - 1, gp, 2, hg, hg), eye2)
    tab = tab.reshape(gp, 2 * hg, (2 * L - 1) * 2 * hg)
    lag_tab = jnp.pad(tab, ((0, 0), (0, 0), (0, 2 * hg)))

    a_f = pw[0][L - 1 - jnp.arange(L)][:, :, None, :] * bbar_t[0][None]
    a_b = pw[1][jnp.arange(L)][:, :, None, :] * bbar_t[1][None]
    loc = jnp.stack([jnp.real(a_f), jnp.imag(a_f), jnp.real(a_b), jnp.imag(a_b)])

    e_f = cc[0][None] * pw[0][1 + jnp.arange(L)][:, :, None, :]
    e_b = cc[1][None] * pw[1][L - jnp.arange(L)][:, :, None, :]
    cr = jnp.stack([jnp.real(e_f), -jnp.imag(e_f), jnp.real(e_b), -jnp.imag(e_b)])

    lam_l = jnp.stack([jnp.real(pw[0][L]), jnp.imag(pw[0][L]),
                       jnp.real(pw[1][L]), jnp.imag(pw[1][L])])
    lam_l = lam_l.reshape(4, gp, 2 * p).transpose(1, 0, 2)
    return lag_tab, loc.reshape(4, L, gp, 2 * hg, p), cr.reshape(4, L, gp, 2 * hg, p), lam_l


def _s5_body(u_ref, tab_ref, loc_ref, cross_ref, lam_ref, y_ref, mintra_ref, mloc_ref, mcross_ref,
             xloc_ref, xin_ref, *, bsz, tr):
    rows, width = u_ref.shape[1], u_ref.shape[2]
    nc = rows // bsz
    w = lam_ref.shape[2]

    tab = tab_ref[0]
    for s in range(S5_CHUNK):
        off = (S5_CHUNK - 1 - s) * S5_SLOT
        mintra_ref[s * S5_SLOT:(s + 1) * S5_SLOT, :] = tab[:, off:off + width].astype(BF16)
    own = (lax.broadcasted_iota(jnp.int32, (S5_SLOT, w), 0) // S5_GROUP
           == lax.broadcasted_iota(jnp.int32, (S5_SLOT, w), 1) // S5_STATE)
    for part in range(4):
        for s in range(S5_CHUNK):
            dst = (slice(s * S5_SLOT, (s + 1) * S5_SLOT), slice(part * w, (part + 1) * w))
            for tab_ref_, m_ref in ((loc_ref, mloc_ref), (cross_ref, mcross_ref)):
                blk = tab_ref_[part, s]
                m_ref[dst] = jnp.where(own, jnp.concatenate([blk, blk], axis=1), 0.0).astype(BF16)

    def loc_fn(i, carry):
        rs = pl.ds(pl.multiple_of(i * tr, tr), tr)
        xloc = jnp.dot(u_ref[0, rs, :], mloc_ref[...], preferred_element_type=F32)
        for part in range(4):
            xloc_ref[part, rs, :] = xloc[:, part * w:(part + 1) * w]
        return carry

    lax.fori_loop(0, rows // tr, loc_fn, 0)

    afr, afi, abr, abi = lam_ref[0, 0:1, :], lam_ref[0, 1:2, :], lam_ref[0, 2:3, :], lam_ref[0, 3:4, :]

    def carry_fn(i, state):
        sfr, sfi, sbr, sbi = state
        rf = pl.ds(pl.multiple_of(i * bsz, bsz), bsz)
        rb = pl.ds(pl.multiple_of((nc - 1 - i) * bsz, bsz), bsz)
        xin_ref[0, rf, :] = sfr
        xin_ref[1, rf, :] = sfi
        xin_ref[2, rb, :] = sbr
        xin_ref[3, rb, :] = sbi
        nfr = afr * sfr - afi * sfi + xloc_ref[0, rf, :]
        nfi = afr * sfi + afi * sfr + xloc_ref[1, rf, :]
        nbr = abr * sbr - abi * sbi + xloc_ref[2, rb, :]
        nbi = abr * sbi + abi * sbr + xloc_ref[3, rb, :]
        return nfr, nfi, nbr, nbi

    zero = jnp.zeros((bsz, w), F32)
    lax.fori_loop(0, nc, carry_fn, (zero, zero, zero, zero), unroll=2)

    def out_fn(i, carry):
        rs = pl.ds(pl.multiple_of(i * tr, tr), tr)
        y = jnp.dot(u_ref[0, rs, :], mintra_ref[...], preferred_element_type=F32)
        xin = jnp.concatenate([xin_ref[part, rs, :] for part in range(4)], axis=1)
        y = y + lax.dot_general(xin.astype(BF16), mcross_ref[...], (((1,), (1,)), ((), ())),
                                preferred_element_type=F32)
        y_ref[0, rs, :] = y
        return carry

    lax.fori_loop(0, rows // tr, out_fn, 0)


def _s5_scan(u_pairs, mats, bsz):
    lag_tab, loc_tab, cross_tab, lam_l = mats
    gp, rows, width = u_pairs.shape
    tr = min(512, rows)
    w = lam_l.shape[2]
    pair_spec = lambda a: pl.BlockSpec((1,) + a.shape[1:], lambda i: (i, 0, 0))
    state_spec = pl.BlockSpec(loc_tab.shape[:2] + (None,) + loc_tab.shape[3:], lambda i: (0, 0, i, 0, 0))
    return pl.pallas_call(
        functools.partial(_s5_body, bsz=bsz, tr=tr),
        grid=(gp,),
        in_specs=[pair_spec(u_pairs), pair_spec(lag_tab), state_spec, state_spec, pair_spec(lam_l)],
        out_specs=pl.BlockSpec((1, rows, width), lambda i: (i, 0, 0)),
        out_shape=jax.ShapeDtypeStruct((gp, rows, width), F32),
        scratch_shapes=[pltpu.VMEM((width, width), BF16), pltpu.VMEM((width, 4 * w), BF16),
                        pltpu.VMEM((width, 4 * w), BF16),
                        pltpu.VMEM((4, rows, w), F32), pltpu.VMEM((4, rows, w), F32)],
        compiler_params=_cparams(("parallel",)),
        name="s5_scan",
    )(u_pairs, lag_tab, loc_tab, cross_tab, lam_l)


def _ab_out_body(x_ref, a_ref, y_ref, wglu_ref, bglu_ref, wout_ref, o_ref, ys_ref):
    bsz, tt, d = x_ref.shape
    half = a_ref.shape[2]
    a = a_ref[...].reshape(bsz * tt, half)
    acc = x_ref[...].reshape(bsz * tt, d) + jnp.dot(a, wout_ref[0:half, :], preferred_element_type=F32)
    lane = lax.broadcasted_iota(jnp.int32, (bsz, LANES), 1)
    for cl in range(tt // S5_CHUNK):
        for v in range(S5_CHUNK // S5_SLOTS):
            for w in range(half // LANES):
                outs = _slot_transpose4([y_ref[S5_SLOTS * w + j, cl, :, v * LANES:(v + 1) * LANES]
                                         for j in range(S5_SLOTS)], lane)
                pos = cl * S5_CHUNK + S5_SLOTS * v
                for i in range(S5_SLOTS):
                    ys_ref[w, pl.ds(pos + i, bsz, stride=tt), :] = outs[i]
    yg = _gelu(jnp.concatenate([ys_ref[w] for w in range(half // LANES)], axis=1))
    gate = _sigmoid(jnp.dot(yg.astype(BF16), wglu_ref[...], preferred_element_type=F32) + bglu_ref[...])
    b_out = (yg * gate).astype(BF16)
    acc = acc + jnp.dot(b_out, wout_ref[half:, :], preferred_element_type=F32)
    o_ref[...] = acc.reshape(bsz, tt, d)


def _ab_out(x, a, y_pairs, wglu, bglu, wout):
    bsz, seq, d = x.shape
    half = a.shape[2]
    pairs = y_pairs.shape[0]
    tt = AB_TOKENS
    tok = lambda n: pl.BlockSpec((bsz, tt, n), lambda i: (0, i, 0))
    return pl.pallas_call(
        _ab_out_body,
        grid=(seq // tt,),
        in_specs=[tok(d), tok(half),
                  pl.BlockSpec((pairs, tt // S5_CHUNK, bsz, half), lambda i: (0, i, 0, 0)),
                  _const_spec(wglu.shape), _const_spec((1, half)), _const_spec(wout.shape)],
        out_specs=tok(d),
        out_shape=jax.ShapeDtypeStruct((bsz, seq, d), F32),
        scratch_shapes=[pltpu.VMEM((half // LANES, bsz * tt, LANES), F32)],
        compiler_params=_cparams(("parallel",)),
        name="ab_out_proj",
    )(x, a, y_pairs, wglu, bglu.reshape(1, half), wout)


def _ffn_body(x_ref, xp_ref, xn_ref, g_ref, wup_ref, cw_ref, cb_ref, wdn_ref, p_ref, pg_ref, wg_ref,
              wp_ref, fg_ref, o_ref, h_ref, act_ref, *, tiles_per_seq, tf, final):
    tm = x_ref.shape[0]
    dff = wdn_ref.shape[0]
    ext = tm + 2 * CONV_HALO
    i = pl.program_id(0)
    has_prev = (i % tiles_per_seq != 0).astype(F32)
    has_next = (i % tiles_per_seq != tiles_per_seq - 1).astype(F32)
    g = g_ref[...]
    h_ref[0:tm, :] = _rms(x_ref[...], g).astype(BF16)
    halo = jnp.concatenate([_rms(xn_ref[...], g) * has_next, _rms(xp_ref[...], g) * has_prev], axis=0)
    h_ref[tm:ext, :] = halo.astype(BF16)
    h = h_ref[...]

    def conv(u, c0):
        w = cw_ref[:, c0:c0 + tf]
        out = (pltpu.roll(u, 1, 0) * w[0:1] + u * w[1:2] + pltpu.roll(u, ext - 1, 0) * w[2:3]
               + cb_ref[:, c0:c0 + tf])
        return out[0:tm]

    for j in range(dff // tf):
        ua = jnp.dot(h, wup_ref[:, j * tf:(j + 1) * tf], preferred_element_type=F32)
        ug = jnp.dot(h, wup_ref[:, dff + j * tf:dff + (j + 1) * tf], preferred_element_type=F32)
        act_ref[:, j * tf:(j + 1) * tf] = (_gelu(conv(ug, dff + j * tf)) * conv(ua, j * tf)).astype(BF16)
    x = x_ref[...] + jnp.dot(act_ref[...], wdn_ref[...], preferred_element_type=F32)
    emb = jnp.dot(p_ref[...].astype(BF16), wp_ref[...], preferred_element_type=F32)
    gate = _sigmoid(jnp.dot(_rms(x, pg_ref[...]).astype(BF16), wg_ref[...], preferred_element_type=F32))
    out = x + gate * emb
    if final:
        out = _rms(out, fg_ref[...])
    o_ref[...] = out


def _ffn_ple(x, seq, g, wup, cw, cb, wdn, p, layer, pg, wg, wp, fg, final, tm=512, tf=256):
    m, d = x.shape
    dff = wdn.shape[0]
    pd = p.shape[2]
    hb = tm // CONV_HALO
    nblk = m // CONV_HALO
    return pl.pallas_call(
        functools.partial(_ffn_body, tiles_per_seq=seq // tm, tf=tf, final=final),
        grid=(m // tm,),
        in_specs=[pl.BlockSpec((tm, d), lambda i: (i, 0)),
                  pl.BlockSpec((CONV_HALO, d), lambda i: (jnp.maximum(i * hb - 1, 0), 0)),
                  pl.BlockSpec((CONV_HALO, d), lambda i: (jnp.minimum((i + 1) * hb, nblk - 1), 0)),
                  _const_spec((1, d)), _const_spec(wup.shape), _const_spec(cw.shape),
                  _const_spec((1, 2 * dff)), _const_spec(wdn.shape),
                  pl.BlockSpec((None, tm, pd), lambda i: (layer, i, 0)),
                  _const_spec((1, d)), _const_spec(wg.shape), _const_spec(wp.shape), _const_spec((1, d))],
        out_specs=pl.BlockSpec((tm, d), lambda i: (i, 0)),
        out_shape=jax.ShapeDtypeStruct((m, d), F32),
        scratch_shapes=[pltpu.VMEM((tm + 2 * CONV_HALO, d), BF16), pltpu.VMEM((tm, dff), BF16)],
        compiler_params=_cparams(("parallel",)),
        name="ffn_ple",
    )(x, x, x, g.reshape(1, d), wup, cw, cb.reshape(1, 2 * dff), wdn, p, pg.reshape(1, d), wg, wp,
      fg.reshape(1, d))


def _ret_in_body(x_ref, g_ref, w_ref, cos_ref, sin_ref, q_ref, k_ref, v_ref, gt_ref):
    h = _rms(x_ref[...], g_ref[...]).astype(BF16)
    cos, sin = cos_ref[...], sin_ref[...]
    half = RET_QK // 2
    dq = q_ref.shape[1]
    dv = v_ref.shape[1]

    def rot(z, scale):
        x1, x2 = z[:, :half], z[:, half:]
        return jnp.concatenate([(x1 * cos - x2 * sin) * scale, (x1 * sin + x2 * cos) * scale], axis=-1)

    for j in range(dq // RET_QK):
        c0 = j * RET_QK
        zq = jnp.dot(h, w_ref[:, c0:c0 + RET_QK], preferred_element_type=F32)
        q_ref[:, c0:c0 + RET_QK] = rot(zq, 1.0).astype(BF16)
        zk = jnp.dot(h, w_ref[:, dq + c0:dq + c0 + RET_QK], preferred_element_type=F32)
        k_ref[:, c0:c0 + RET_QK] = rot(zk, RET_QK ** -0.5).astype(BF16)
    tn = 512
    for j in range(dv // tn):
        c0 = j * tn
        v_ref[:, c0:c0 + tn] = jnp.dot(h, w_ref[:, 2 * dq + c0:2 * dq + c0 + tn],
                                       preferred_element_type=F32).astype(BF16)
        gate = jnp.dot(h, w_ref[:, 2 * dq + dv + c0:2 * dq + dv + c0 + tn], preferred_element_type=F32)
        gt_ref[:, c0:c0 + tn] = (gate * _sigmoid(gate)).astype(BF16)


def _ret_in_proj(x, seq, g, w, cos, sin, tm=512):
    m, d = x.shape
    dq = RET_HEADS * RET_QK
    dv = RET_HEADS * RET_V
    tps = seq // tm
    row = lambda n: pl.BlockSpec((tm, n), lambda i: (i, 0))
    pos = pl.BlockSpec((tm, RET_QK // 2), lambda i: (i % tps, 0))
    return pl.pallas_call(
        _ret_in_body,
        grid=(m // tm,),
        in_specs=[row(d), _const_spec((1, d)), _const_spec(w.shape), pos, pos],
        out_specs=[row(dq), row(dq), row(dv), row(dv)],
        out_shape=[jax.ShapeDtypeStruct((m, dq), BF16), jax.ShapeDtypeStruct((m, dq), BF16),
                   jax.ShapeDtypeStruct((m, dv), BF16), jax.ShapeDtypeStruct((m, dv), BF16)],
        compiler_params=_cparams(("parallel",)),
        name="ret_in_proj",
    )(x, g.reshape(1, d), w, cos, sin)


def _ret_tables(decay_param):
    L = RET_CHUNK
    lg = -jnp.exp(decay_param.astype(F32))
    lg_f, lg_b = lg[0][:, None], lg[1][:, None]
    pos = jnp.arange(L, dtype=F32)
    diff = pos[:, None] - pos[None, :]
    dmat = jnp.where(diff >= 0, jnp.exp(lg_f[:, :, None] * jnp.abs(diff)),
                     jnp.exp(lg_b[:, :, None] * jnp.abs(diff)))
    vecs = jnp.stack([jnp.exp(lg_f * (pos + 1.0)),
                      jnp.exp(lg_f * (L - 1.0 - pos)),
                      jnp.exp(lg_b * (L - pos)),
                      jnp.exp(lg_b * pos)], axis=-1)
    cdec = jnp.exp(lg * L).T
    return dmat, vecs, cdec


def _ret_body(cdec_ref, q_ref, k_ref, v_ref, gt_ref, dmat_ref, vec_ref, o_ref, sb_ref, st_ref):
    L = RET_CHUNK
    seq = q_ref.shape[1]
    n = seq // L
    hd = pl.program_id(1)
    cf = cdec_ref[hd, 0]
    cb = cdec_ref[hd, 1]
    vec = vec_ref[0]
    qf, kf, qb, kb = vec[:, 0:1], vec[:, 1:2], vec[:, 2:3], vec[:, 3:4]
    contract0 = (((0,), (0,)), ((), ()))

    def scaled(t, col):
        return (t.astype(F32) * col).astype(BF16)

    st_ref[...] = jnp.zeros_like(st_ref)

    def bwd(i, carry):
        c = n - 1 - i
        rs = pl.ds(pl.multiple_of(c * L, L), L)
        sb_ref[c] = st_ref[...].astype(BF16)
        st_ref[...] = cb * st_ref[...] + lax.dot_general(scaled(k_ref[0, rs, :], kb), v_ref[0, rs, :],
                                                         contract0, preferred_element_type=F32)
        return carry

    lax.fori_loop(0, n, bwd, 0, unroll=8)

    st_ref[...] = jnp.zeros_like(st_ref)

    def fwd(c, carry):
        rs = pl.ds(pl.multiple_of(c * L, L), L)
        q_c, k_c, v_c = q_ref[0, rs, :], k_ref[0, rs, :], v_ref[0, rs, :]
        scores = lax.dot_general(q_c, k_c, (((1,), (1,)), ((), ())), preferred_element_type=F32)
        y = jnp.dot((scores * dmat_ref[0]).astype(BF16), v_c, preferred_element_type=F32)
        y = y + jnp.dot(scaled(q_c, qf), st_ref[...].astype(BF16), preferred_element_type=F32)
        y = y + jnp.dot(scaled(q_c, qb), sb_ref[c], preferred_element_type=F32)
        st_ref[...] = cf * st_ref[...] + lax.dot_general(scaled(k_c, kf), v_c, contract0,
                                                         preferred_element_type=F32)
        y = y * lax.rsqrt(jnp.mean(y * y, axis=-1, keepdims=True) + EPS)
        o_ref[0, rs, :] = (gt_ref[0, rs, :].astype(F32) * y).astype(o_ref.dtype)
        return carry

    lax.fori_loop(0, n, fwd, 0, unroll=8)


def _retention(q, k, v, gt, tables):
    dmat, vecs, cdec = tables
    b, s, _ = q.shape
    L = RET_CHUNK
    qk_spec = pl.BlockSpec((1, s, RET_QK), lambda i, h, *_: (i, 0, h))
    v_spec = pl.BlockSpec((1, s, RET_V), lambda i, h, *_: (i, 0, h))
    grid_spec = pltpu.PrefetchScalarGridSpec(
        num_scalar_prefetch=1,
        grid=(b, RET_HEADS),
        in_specs=[qk_spec, qk_spec, v_spec, v_spec,
                  pl.BlockSpec((1, L, L), lambda i, h, *_: (h, 0, 0)),
                  pl.BlockSpec((1, L, 4), lambda i, h, *_: (h, 0, 0))],
        out_specs=v_spec,
        scratch_shapes=[pltpu.VMEM((s // L, RET_QK, RET_V), BF16), pltpu.VMEM((RET_QK, RET_V), F32)],
    )
    return pl.pallas_call(
        _ret_body,
        grid_spec=grid_spec,
        out_shape=jax.ShapeDtypeStruct((b, s, RET_HEADS * RET_V), BF16),
        compiler_params=_cparams(("parallel", "parallel")),
        name="retention",
    )(cdec, q, k, v, gt, dmat, vecs)


def _proj_residual_body(x_ref, y_ref, w_ref, o_ref):
    o_ref[...] = x_ref[...] + jnp.dot(y_ref[...], w_ref[...], preferred_element_type=F32)


def _proj_residual(x, y, w, tm=512):
    m, d = x.shape
    kd = y.shape[1]
    return pl.pallas_call(
        _proj_residual_body,
        grid=(m // tm,),
        in_specs=[pl.BlockSpec((tm, d), lambda i: (i, 0)), pl.BlockSpec((tm, kd), lambda i: (i, 0)),
                  _const_spec(w.shape)],
        out_specs=pl.BlockSpec((tm, d), lambda i: (i, 0)),
        out_shape=jax.ShapeDtypeStruct((m, d), F32),
        compiler_params=_cparams(("parallel",)),
        name="ret_out_proj",
    )(x, y, w)


def _na_s5_layer(x, bsz, seq, norm, w_in, rpb, lam_re, lam_im, log_dt, b_re, b_im, c_re, c_im, d_skip,
                 w_glu, b_glu, w_out):
    m, d = x.shape
    half = NA_HEADS * NA_HEAD_DIM
    col_scale = jnp.concatenate([jnp.full((half,), NA_HEAD_DIM ** -0.5 * LOG2E, F32),
                                 jnp.ones((3 * half,), F32)])
    z, u_pairs = _ab_in_proj(x.reshape(bsz, seq, d), norm, (w_in * col_scale).astype(BF16), 3 * half)
    a_out = _na_attention(z, _na_bias_table(rpb))
    mats = _s5_matrices(lam_re, lam_im, log_dt, b_re, b_im, c_re, c_im, d_skip)
    pairs, nc = u_pairs.shape[0], u_pairs.shape[1]
    y_pairs = _s5_scan(u_pairs.reshape(pairs, nc * bsz, half), mats, bsz).reshape(pairs, nc, bsz, half)
    out = _ab_out(x.reshape(bsz, seq, d), a_out, y_pairs, w_glu.astype(BF16), b_glu, w_out.astype(BF16))
    return out.reshape(m, d)


def _rope_tables(seq):
    half = RET_QK // 2
    inv_freq = ROPE_BASE ** (-jnp.arange(half, dtype=F32) / half)
    ang = jnp.arange(seq, dtype=F32)[:, None] * inv_freq[None, :]
    return jnp.cos(ang), jnp.sin(ang)


def _retention_layer(x, bsz, seq, norm, w_in, decay, w_out):
    m, d = x.shape
    cos, sin = _rope_tables(seq)
    q, k, v, gt = _ret_in_proj(x, seq, norm, w_in.astype(BF16), cos, sin)
    shp = lambda t: t.reshape(bsz, seq, t.shape[-1])
    y = _retention(shp(q), shp(k), shp(v), shp(gt), _ret_tables(decay))
    return _proj_residual(x, y.reshape(m, -1), w_out.astype(BF16))


def kernel(x, p, ab_norm, ab_w_in, na_rpb, s5_lambda_re, s5_lambda_im, s5_log_dt, s5_b_re, s5_b_im,
           s5_c_re, s5_c_im, s5_d, s5_w_glu, s5_b_glu, ab_w_out, ret_norm, ret_w_in, ret_decay, ret_w_out,
           ffn_norm, ffn_w_up, ffn_conv_w, ffn_conv_b, ffn_w_down, ple_norm, ple_w_gate, ple_w_proj,
           final_norm):
    bsz, seq, d = x.shape
    depth = p.shape[0]
    m = bsz * seq
    xs = x.reshape(m, d)
    p_tok = p.reshape(depth, m, p.shape[-1])
    for i in range(depth):
        j = i // 2
        if i % 2 == 0:
            xs = _na_s5_layer(xs, bsz, seq, ab_norm[j], ab_w_in[j], na_rpb[j], s5_lambda_re[j],
                              s5_lambda_im[j], s5_log_dt[j], s5_b_re[j], s5_b_im[j], s5_c_re[j],
                              s5_c_im[j], s5_d[j], s5_w_glu[j], s5_b_glu[j], ab_w_out[j])
        else:
            xs = _retention_layer(xs, bsz, seq, ret_norm[j], ret_w_in[j], ret_decay[j], ret_w_out[j])
        xs = _ffn_ple(xs, seq, ffn_norm[i], ffn_w_up[i].astype(BF16), ffn_conv_w[i], ffn_conv_b[i],
                      ffn_w_down[i].astype(BF16), p_tok, i, ple_norm[i], ple_w_gate[i].astype(BF16),
                      ple_w_proj[i].astype(BF16), final_norm, final=(i == depth - 1))
    return xs.reshape(bsz, seq, d)
```

```python
import functools

import jax
import jax.numpy as jnp
from jax import lax
from jax.experimental import pallas as pl
from jax.experimental.pallas import tpu as pltpu

F32 = jnp.float32
BF16 = jnp.bfloat16

LANES = 128
EPS = 1e-6
GRID_W = 64
NA_HEADS = 8
NA_HEAD_DIM = 64
NA_WIN_ROWS = 8
NA_WIN_COLS = 16
NA_MASK = -1e30
LOG2E = 1.4426950408889634
NA_HEAD_BLOCK = 4
S5_GROUP = 16
S5_STATE = 64
S5_CHUNK = 16
S5_SLOT = 2 * S5_GROUP
S5_SLOTS = LANES // S5_SLOT
AB_TOKENS = 64
RET_HEADS = 4
RET_QK = 256
RET_V = 512
RET_CHUNK = 256
ROPE_BASE = 10000.0
CONV_HALO = 8
VMEM_LIMIT = 56 * 1024 * 1024


def _cparams(sem):
    return pltpu.CompilerParams(dimension_semantics=sem, vmem_limit_bytes=VMEM_LIMIT)


def _const_spec(shape):
    nd = len(shape)
    return pl.BlockSpec(shape, lambda *_: (0,) * nd, pipeline_mode=pl.Buffered(1))


def _rms(xf, g):
    ms = jnp.mean(xf * xf, axis=-1, keepdims=True)
    return xf * lax.rsqrt(ms + EPS) * g


def _gelu(x):
    return 0.5 * x * (1.0 + lax.erf(x * (2.0 ** -0.5)))


def _sigmoid(x):
    return 0.5 * jnp.tanh(0.5 * x) + 0.5


def _slot_transpose4(a, lane):
    lo = lane < 2 * S5_SLOT
    even = (lane % (2 * S5_SLOT)) < S5_SLOT
    b0 = jnp.where(lo, a[0], pltpu.roll(a[2], 2 * S5_SLOT, 1))
    b2 = jnp.where(lo, pltpu.roll(a[0], 2 * S5_SLOT, 1), a[2])
    b1 = jnp.where(lo, a[1], pltpu.roll(a[3], 2 * S5_SLOT, 1))
    b3 = jnp.where(lo, pltpu.roll(a[1], 2 * S5_SLOT, 1), a[3])
    return [jnp.where(even, b0, pltpu.roll(b1, S5_SLOT, 1)),
            jnp.where(even, pltpu.roll(b0, LANES - S5_SLOT, 1), b1),
            jnp.where(even, b2, pltpu.roll(b3, S5_SLOT, 1)),
            jnp.where(even, pltpu.roll(b2, LANES - S5_SLOT, 1), b3)]


def _ab_in_body(x_ref, g_ref, w_ref, z_ref, u_ref, zs_ref, *, tn):
    bsz, tt, d = x_ref.shape
    h = _rms(x_ref[...].reshape(bsz * tt, d), g_ref[...]).astype(BF16)
    nz = z_ref.shape[2]
    zu = jnp.dot(h, w_ref[:, nz:], preferred_element_type=F32)
    for w in range(zs_ref.shape[0]):
        zs_ref[w] = zu[:, w * LANES:(w + 1) * LANES]
    lane = lax.broadcasted_iota(jnp.int32, (bsz, LANES), 1)
    for cl in range(tt // S5_CHUNK):
        for v in range(S5_CHUNK // S5_SLOTS):
            for w in range(zs_ref.shape[0]):
                pos = cl * S5_CHUNK + S5_SLOTS * v
                outs = _slot_transpose4([zs_ref[w, pl.ds(pos + i, bsz, stride=tt), :]
                                         for i in range(S5_SLOTS)], lane)
                for j in range(S5_SLOTS):
                    u_ref[S5_SLOTS * w + j, cl, :, v * LANES:(v + 1) * LANES] = outs[j].astype(u_ref.dtype)
    for j in range(nz // tn):
        z_ref[:, :, j * tn:(j + 1) * tn] = jnp.dot(
            h, w_ref[:, j * tn:(j + 1) * tn], preferred_element_type=F32
        ).astype(z_ref.dtype).reshape(bsz, tt, tn)


def _ab_in_proj(x, g, w, nz, tn=512):
    bsz, seq, d = x.shape
    nu = w.shape[1] - nz
    pairs = nu // S5_SLOT
    assert S5_CHUNK * S5_SLOT == nu and pairs == (nu // LANES) * S5_SLOTS
    tt = AB_TOKENS
    return pl.pallas_call(
        functools.partial(_ab_in_body, tn=tn),
        grid=(seq // tt,),
        in_specs=[pl.BlockSpec((bsz, tt, d), lambda i: (0, i, 0)), _const_spec((1, d)), _const_spec(w.shape)],
        out_specs=[pl.BlockSpec((bsz, tt, nz), lambda i: (0, i, 0)),
                   pl.BlockSpec((pairs, tt // S5_CHUNK, bsz, nu), lambda i: (0, i, 0, 0))],
        out_shape=[jax.ShapeDtypeStruct((bsz, seq, nz), BF16),
                   jax.ShapeDtypeStruct((pairs, seq // S5_CHUNK, bsz, nu), BF16)],
        scratch_shapes=[pltpu.VMEM((nu // LANES, bsz * tt, LANES), F32)],
        compiler_params=_cparams(("parallel",)),
        name="ab_in_proj",
    )(x, g.reshape(1, d), w)


def _na_bias_table(rpb):
    cols = jnp.arange(GRID_W)
    cs = jnp.clip(cols - NA_WIN_COLS // 2, 0, GRID_W - NA_WIN_COLS)
    j = jnp.arange(GRID_W)
    inwin = (j[None, :] >= cs[:, None]) & (j[None, :] < cs[:, None] + NA_WIN_COLS)
    dc = jnp.clip(j[None, :] - cols[:, None] + NA_WIN_COLS - 1, 0, 2 * NA_WIN_COLS - 2)
    t = (rpb.astype(F32) * LOG2E)[:, :, dc]
    t = jnp.where(inwin[None, None], t, NA_MASK).transpose(0, 2, 1, 3)
    h = rpb.shape[0]
    kw = NA_WIN_ROWS * GRID_W
    return jnp.stack([t[:, :, v:v + NA_WIN_ROWS, :].reshape(h // NA_HEAD_BLOCK, NA_HEAD_BLOCK * GRID_W, kw)
                      for v in range(NA_WIN_ROWS)])


def _na_body(q_ref, k_ref, v_ref, bias_ref, o_ref, *, rows):
    kwin = NA_WIN_ROWS * GRID_W
    bw = NA_HEAD_BLOCK * NA_HEAD_DIM
    head_of_lane = lax.broadcasted_iota(jnp.int32, (GRID_W, bw), 1) // NA_HEAD_DIM

    def row_fn(r, carry):
        rs = jnp.clip(r - NA_WIN_ROWS // 2, 0, rows - NA_WIN_ROWS)
        variant = rs - r + (NA_WIN_ROWS - 1)
        qrow = pl.ds(pl.multiple_of(r * GRID_W, GRID_W), GRID_W)
        krow = pl.ds(pl.multiple_of(rs * GRID_W, GRID_W), kwin)
        for blk in range(NA_HEADS // NA_HEAD_BLOCK):
            sl = slice(blk * bw, (blk + 1) * bw)
            q_p, k_p, v_p = q_ref[0, qrow, sl], k_ref[0, krow, sl], v_ref[0, krow, sl]
            q_all = jnp.concatenate([jnp.where(head_of_lane == hh, q_p, jnp.zeros_like(q_p))
                                     for hh in range(NA_HEAD_BLOCK)], axis=0)
            logits = lax.dot_general(q_all, k_p, (((1,), (1,)), ((), ())), preferred_element_type=F32)
            logits = logits + bias_ref[variant, blk]
            e = jnp.exp2(logits - jnp.max(logits, axis=-1, keepdims=True))
            den = jnp.sum(e, axis=-1, keepdims=True)
            o = jnp.dot(e.astype(BF16), v_p, preferred_element_type=F32) / den
            out = o[0:GRID_W]
            for hh in range(1, NA_HEAD_BLOCK):
                out = jnp.where(head_of_lane == hh, o[hh * GRID_W:(hh + 1) * GRID_W], out)
            o_ref[0, qrow, sl] = out.astype(o_ref.dtype)
        return carry

    lax.fori_loop(0, rows, row_fn, 0, unroll=16)


def _na_attention(z, bias):
    b, s, _ = z.shape
    width = NA_HEADS * NA_HEAD_DIM
    rows = s // GRID_W
    spec = lambda c: pl.BlockSpec((1, s, width), lambda i, c=c: (i, 0, c))
    return pl.pallas_call(
        functools.partial(_na_body, rows=rows),
        grid=(b,),
        in_specs=[spec(0), spec(1), spec(2), _const_spec(bias.shape)],
        out_specs=pl.BlockSpec((1, s, width), lambda i: (i, 0, 0)),
        out_shape=jax.ShapeDtypeStruct((b, s, width), BF16),
        compiler_params=_cparams(("parallel",)),
        name="na_attention",
    )(z, z, z, bias)


def _s5_matrices(lam_re, lam_im, log_dt, b_re, b_im, c_re, c_im, d_skip):
    L, hg, p = S5_CHUNK, S5_GROUP, S5_STATE
    g = lam_re.shape[1]
    gp = g // 2
    tau = jnp.arange(L + 1, dtype=F32)
    eye2 = jnp.eye(2, dtype=F32)
    pw, bbar_t, cc = [], [], []
    for d in range(2):
        lam = lax.complex(lam_re[d].astype(F32), lam_im[d].astype(F32))
        lam_dt = lam * jnp.exp(log_dt[d].astype(F32))[:, None]
        lam_bar = jnp.exp(lam_dt)
        pw.append(jnp.exp(lam_dt[None] * tau[:, None, None]))
        b_c = lax.complex(b_re[d].astype(F32), b_im[d].astype(F32))
        bbar_t.append((((lam_bar - 1.0) / lam)[:, :, None] * b_c).transpose(0, 2, 1))
        cc.append(lax.complex(c_re[d].astype(F32), c_im[d].astype(F32)))

    kf = jnp.real(jnp.einsum('gnp,tgp,ghp->tghn', cc[0], pw[0][:L], bbar_t[0]))
    kb = jnp.real(jnp.einsum('gnp,tgp,ghp->tghn', cc[1], pw[1][:L], bbar_t[1]))
    skip = jnp.eye(hg, dtype=F32)[None] * d_skip.astype(F32).reshape(g, 1, hg)
    tab = jnp.concatenate([kb[1:][::-1], (kf[0] + kb[0] + skip)[None], kf[1:]])
    tab = jnp.einsum('aqihn,ij->qihajn', tab.reshape(2 * L - 1, gp, 2, hg, hg), eye2)
    tab = tab.reshape(gp, 2 * hg, (2 * L - 1) * 2 * hg)
    lag_tab = jnp.pad(tab, ((0, 0), (0, 0), (0, 2 * hg)))

    a_f = pw[0][L - 1 - jnp.arange(L)][:, :, None, :] * bbar_t[0][None]
    a_b = pw[1][jnp.arange(L)][:, :, None, :] * bbar_t[1][None]
    loc = jnp.stack([jnp.real(a_f), jnp.imag(a_f), jnp.real(a_b), jnp.imag(a_b)])

    e_f = cc[0][None] * pw[0][1 + jnp.arange(L)][:, :, None, :]
    e_b = cc[1][None] * pw[1][L - jnp.arange(L)][:, :, None, :]
    cr = jnp.stack([jnp.real(e_f), -jnp.imag(e_f), jnp.real(e_b), -jnp.imag(e_b)])

    lam_l = jnp.stack([jnp.real(pw[0][L]), jnp.imag(pw[0][L]),
                       jnp.real(pw[1][L]), jnp.imag(pw[1][L])])
    lam_l = lam_l.reshape(4, gp, 2 * p).transpose(1, 0, 2)
    return lag_tab, loc.reshape(4, L, gp, 2 * hg, p), cr.reshape(4, L, gp, 2 * hg, p), lam_l


def _s5_body(u_ref, tab_ref, loc_ref, cross_ref, lam_ref, y_ref, mintra_ref, mloc_ref, mcross_ref,
             xloc_ref, xin_ref, *, bsz, tr):
    rows, width = u_ref.shape[1], u_ref.shape[2]
    nc = rows // bsz
    w = lam_ref.shape[2]

    tab = tab_ref[0]
    for s in range(S5_CHUNK):
        off = (S5_CHUNK - 1 - s) * S5_SLOT
        mintra_ref[s * S5_SLOT:(s + 1) * S5_SLOT, :] = tab[:, off:off + width].astype(BF16)
    own = (lax.broadcasted_iota(jnp.int32, (S5_SLOT, w), 0) // S5_GROUP
           == lax.broadcasted_iota(jnp.int32, (S5_SLOT, w), 1) // S5_STATE)
    for part in range(4):
        for s in range(S5_CHUNK):
            dst = (slice(s * S5_SLOT, (s + 1) * S5_SLOT), slice(part * w, (part + 1) * w))
            for tab_ref_, m_ref in ((loc_ref, mloc_ref), (cross_ref, mcross_ref)):
                blk = tab_ref_[part, s]
                m_ref[dst] = jnp.where(own, jnp.concatenate([blk, blk], axis=1), 0.0).astype(BF16)

    def loc_fn(i, carry):
        rs = pl.ds(pl.multiple_of(i * tr, tr), tr)
        xloc = jnp.dot(u_ref[0, rs, :], mloc_ref[...], preferred_element_type=F32)
        for part in range(4):
            xloc_ref[part, rs, :] = xloc[:, part * w:(part + 1) * w]
        return carry

    lax.fori_loop(0, rows // tr, loc_fn, 0)

    afr, afi, abr, abi = lam_ref[0, 0:1, :], lam_ref[0, 1:2, :], lam_ref[0, 2:3, :], lam_ref[0, 3:4, :]

    def carry_fn(i, state):
        sfr, sfi, sbr, sbi = state
        rf = pl.ds(pl.multiple_of(i * bsz, bsz), bsz)
        rb = pl.ds(pl.multiple_of((nc - 1 - i) * bsz, bsz), bsz)
        xin_ref[0, rf, :] = sfr
        xin_ref[1, rf, :] = sfi
        xin_ref[2, rb, :] = sbr
        xin_ref[3, rb, :] = sbi
        nfr = afr * sfr - afi * sfi + xloc_ref[0, rf, :]
        nfi = afr * sfi + afi * sfr + xloc_ref[1, rf, :]
        nbr = abr * sbr - abi * sbi + xloc_ref[2, rb, :]
        nbi = abr * sbi + abi * sbr + xloc_ref[3, rb, :]
        return nfr, nfi, nbr, nbi

    zero = jnp.zeros((bsz, w), F32)
    lax.fori_loop(0, nc, carry_fn, (zero, zero, zero, zero), unroll=2)

    def out_fn(i, carry):
        rs = pl.ds(pl.multiple_of(i * tr, tr), tr)
        y = jnp.dot(u_ref[0, rs, :], mintra_ref[...], preferred_element_type=F32)
        xin = jnp.concatenate([xin_ref[part, rs, :] for part in range(4)], axis=1)
        y = y + lax.dot_general(xin.astype(BF16), mcross_ref[...], (((1,), (1,)), ((), ())),
                                preferred_element_type=F32)
        y_ref[0, rs, :] = y
        return carry

    lax.fori_loop(0, rows // tr, out_fn, 0)


def _s5_scan(u_pairs, mats, bsz):
    lag_tab, loc_tab, cross_tab, lam_l = mats
    gp, rows, width = u_pairs.shape
    tr = min(512, rows)
    w = lam_l.shape[2]
    pair_spec = lambda a: pl.BlockSpec((1,) + a.shape[1:], lambda i: (i, 0, 0))
    state_spec = pl.BlockSpec(loc_tab.shape[:2] + (None,) + loc_tab.shape[3:], lambda i: (0, 0, i, 0, 0))
    return pl.pallas_call(
        functools.partial(_s5_body, bsz=bsz, tr=tr),
        grid=(gp,),
        in_specs=[pair_spec(u_pairs), pair_spec(lag_tab), state_spec, state_spec, pair_spec(lam_l)],
        out_specs=pl.BlockSpec((1, rows, width), lambda i: (i, 0, 0)),
        out_shape=jax.ShapeDtypeStruct((gp, rows, width), F32),
        scratch_shapes=[pltpu.VMEM((width, width), BF16), pltpu.VMEM((width, 4 * w), BF16),
                        pltpu.VMEM((width, 4 * w), BF16),
                        pltpu.VMEM((4, rows, w), F32), pltpu.VMEM((4, rows, w), F32)],
        compiler_params=_cparams(("parallel",)),
        name="s5_scan",
    )(u_pairs, lag_tab, loc_tab, cross_tab, lam_l)


def _ab_out_body(x_ref, a_ref, y_ref, wglu_ref, bglu_ref, wout_ref, o_ref, ys_ref):
    bsz, tt, d = x_ref.shape
    half = a_ref.shape[2]
    a = a_ref[...].reshape(bsz * tt, half)
    acc = x_ref[...].reshape(bsz * tt, d) + jnp.dot(a, wout_ref[0:half, :], preferred_element_type=F32)
    lane = lax.broadcasted_iota(jnp.int32, (bsz, LANES), 1)
    for cl in range(tt // S5_CHUNK):
        for v in range(S5_CHUNK // S5_SLOTS):
            for w in range(half // LANES):
                outs = _slot_transpose4([y_ref[S5_SLOTS * w + j, cl, :, v * LANES:(v + 1) * LANES]
                                         for j in range(S5_SLOTS)], lane)
                pos = cl * S5_CHUNK + S5_SLOTS * v
                for i in range(S5_SLOTS):
                    ys_ref[w, pl.ds(pos + i, bsz, stride=tt), :] = outs[i]
    yg = _gelu(jnp.concatenate([ys_ref[w] for w in range(half // LANES)], axis=1))
    gate = _sigmoid(jnp.dot(yg.astype(BF16), wglu_ref[...], preferred_element_type=F32) + bglu_ref[...])
    b_out = (yg * gate).astype(BF16)
    acc = acc + jnp.dot(b_out, wout_ref[half:, :], preferred_element_type=F32)
    o_ref[...] = acc.reshape(bsz, tt, d)


def _ab_out(x, a, y_pairs, wglu, bglu, wout):
    bsz, seq, d = x.shape
    half = a.shape[2]
    pairs = y_pairs.shape[0]
    tt = AB_TOKENS
    tok = lambda n: pl.BlockSpec((bsz, tt, n), lambda i: (0, i, 0))
    return pl.pallas_call(
        _ab_out_body,
        grid=(seq // tt,),
        in_specs=[tok(d), tok(half),
                  pl.BlockSpec((pairs, tt // S5_CHUNK, bsz, half), lambda i: (0, i, 0, 0)),
                  _const_spec(wglu.shape), _const_spec((1, half)), _const_spec(wout.shape)],
        out_specs=tok(d),
        out_shape=jax.ShapeDtypeStruct((bsz, seq, d), F32),
        scratch_shapes=[pltpu.VMEM((half // LANES, bsz * tt, LANES), F32)],
        compiler_params=_cparams(("parallel",)),
        name="ab_out_proj",
    )(x, a, y_pairs, wglu, bglu.reshape(1, half), wout)


def _ffn_body(x_ref, xp_ref, xn_ref, g_ref, wup_ref, cw_ref, cb_ref, wdn_ref, p_ref, pg_ref, wg_ref,
              wp_ref, fg_ref, o_ref, h_ref, act_ref, *, tiles_per_seq, tf, final):
    tm = x_ref.shape[0]
    dff = wdn_ref.shape[0]
    ext = tm + 2 * CONV_HALO
    i = pl.program_id(0)
    has_prev = (i % tiles_per_seq != 0).astype(F32)
    has_next = (i % tiles_per_seq != tiles_per_seq - 1).astype(F32)
    g = g_ref[...]
    h_ref[0:tm, :] = _rms(x_ref[...], g).astype(BF16)
    halo = jnp.concatenate([_rms(xn_ref[...], g) * has_next, _rms(xp_ref[...], g) * has_prev], axis=0)
    h_ref[tm:ext, :] = halo.astype(BF16)
    h = h_ref[...]

    def conv(u, c0):
        w = cw_ref[:, c0:c0 + tf]
        out = (pltpu.roll(u, 1, 0) * w[0:1] + u * w[1:2] + pltpu.roll(u, ext - 1, 0) * w[2:3]
               + cb_ref[:, c0:c0 + tf])
        return out[0:tm]

    for j in range(dff // tf):
        ua = jnp.dot(h, wup_ref[:, j * tf:(j + 1) * tf], preferred_element_type=F32)
        ug = jnp.dot(h, wup_ref[:, dff + j * tf:dff + (j + 1) * tf], preferred_element_type=F32)
        act_ref[:, j * tf:(j + 1) * tf] = (_gelu(conv(ug, dff + j * tf)) * conv(ua, j * tf)).astype(BF16)
    x = x_ref[...] + jnp.dot(act_ref[...], wdn_ref[...], preferred_element_type=F32)
    emb = jnp.dot(p_ref[...].astype(BF16), wp_ref[...], preferred_element_type=F32)
    gate = _sigmoid(jnp.dot(_rms(x, pg_ref[...]).astype(BF16), wg_ref[...], preferred_element_type=F32))
    out = x + gate * emb
    if final:
        out = _rms(out, fg_ref[...])
    o_ref[...] = out


def _ffn_ple(x, seq, g, wup, cw, cb, wdn, p, layer, pg, wg, wp, fg, final, tm=512, tf=256):
    m, d = x.shape
    dff = wdn.shape[0]
    pd = p.shape[2]
    hb = tm // CONV_HALO
    nblk = m // CONV_HALO
    return pl.pallas_call(
        functools.partial(_ffn_body, tiles_per_seq=seq // tm, tf=tf, final=final),
        grid=(m // tm,),
        in_specs=[pl.BlockSpec((tm, d), lambda i: (i, 0)),
                  pl.BlockSpec((CONV_HALO, d), lambda i: (jnp.maximum(i * hb - 1, 0), 0)),
                  pl.BlockSpec((CONV_HALO, d), lambda i: (jnp.minimum((i + 1) * hb, nblk - 1), 0)),
                  _const_spec((1, d)), _const_spec(wup.shape), _const_spec(cw.shape),
                  _const_spec((1, 2 * dff)), _const_spec(wdn.shape),
                  pl.BlockSpec((None, tm, pd), lambda i: (layer, i, 0)),
                  _const_spec((1, d)), _const_spec(wg.shape), _const_spec(wp.shape), _const_spec((1, d))],
        out_specs=pl.BlockSpec((tm, d), lambda i: (i, 0)),
        out_shape=jax.ShapeDtypeStruct((m, d), F32),
        scratch_shapes=[pltpu.VMEM((tm + 2 * CONV_HALO, d), BF16), pltpu.VMEM((tm, dff), BF16)],
        compiler_params=_cparams(("parallel",)),
        name="ffn_ple",
    )(x, x, x, g.reshape(1, d), wup, cw, cb.reshape(1, 2 * dff), wdn, p, pg.reshape(1, d), wg, wp,
      fg.reshape(1, d))


def _ret_in_body(x_ref, g_ref, w_ref, cos_ref, sin_ref, q_ref, k_ref, v_ref, gt_ref):
    h = _rms(x_ref[...], g_ref[...]).astype(BF16)
    cos, sin = cos_ref[...], sin_ref[...]
    half = RET_QK // 2
    dq = q_ref.shape[1]
    dv = v_ref.shape[1]

    def rot(z, scale):
        x1, x2 = z[:, :half], z[:, half:]
        return jnp.concatenate([(x1 * cos - x2 * sin) * scale, (x1 * sin + x2 * cos) * scale], axis=-1)

    for j in range(dq // RET_QK):
        c0 = j * RET_QK
        zq = jnp.dot(h, w_ref[:, c0:c0 + RET_QK], preferred_element_type=F32)
        q_ref[:, c0:c0 + RET_QK] = rot(zq, 1.0).astype(BF16)
        zk = jnp.dot(h, w_ref[:, dq + c0:dq + c0 + RET_QK], preferred_element_type=F32)
        k_ref[:, c0:c0 + RET_QK] = rot(zk, RET_QK ** -0.5).astype(BF16)
    tn = 512
    for j in range(dv // tn):
        c0 = j * tn
        v_ref[:, c0:c0 + tn] = jnp.dot(h, w_ref[:, 2 * dq + c0:2 * dq + c0 + tn],
                                       preferred_element_type=F32).astype(BF16)
        gate = jnp.dot(h, w_ref[:, 2 * dq + dv + c0:2 * dq + dv + c0 + tn], preferred_element_type=F32)
        gt_ref[:, c0:c0 + tn] = (gate * _sigmoid(gate)).astype(BF16)


def _ret_in_proj(x, seq, g, w, cos, sin, tm=512):
    m, d = x.shape
    dq = RET_HEADS * RET_QK
    dv = RET_HEADS * RET_V
    tps = seq // tm
    row = lambda n: pl.BlockSpec((tm, n), lambda i: (i, 0))
    pos = pl.BlockSpec((tm, RET_QK // 2), lambda i: (i % tps, 0))
    return pl.pallas_call(
        _ret_in_body,
        grid=(m // tm,),
        in_specs=[row(d), _const_spec((1, d)), _const_spec(w.shape), pos, pos],
        out_specs=[row(dq), row(dq), row(dv), row(dv)],
        out_shape=[jax.ShapeDtypeStruct((m, dq), BF16), jax.ShapeDtypeStruct((m, dq), BF16),
                   jax.ShapeDtypeStruct((m, dv), BF16), jax.ShapeDtypeStruct((m, dv), BF16)],
        compiler_params=_cparams(("parallel",)),
        name="ret_in_proj",
    )(x, g.reshape(1, d), w, cos, sin)


def _ret_tables(decay_param):
    L = RET_CHUNK
    lg = -jnp.exp(decay_param.astype(F32))
    lg_f, lg_b = lg[0][:, None], lg[1][:, None]
    pos = jnp.arange(L, dtype=F32)
    diff = pos[:, None] - pos[None, :]
    dmat = jnp.where(diff >= 0, jnp.exp(lg_f[:, :, None] * jnp.abs(diff)),
                     jnp.exp(lg_b[:, :, None] * jnp.abs(diff)))
    vecs = jnp.stack([jnp.exp(lg_f * (pos + 1.0)),
                      jnp.exp(lg_f * (L - 1.0 - pos)),
                      jnp.exp(lg_b * (L - pos)),
                      jnp.exp(lg_b * pos)], axis=-1)
    cdec = jnp.exp(lg * L).T
    return dmat, vecs, cdec


def _ret_body(cdec_ref, q_ref, k_ref, v_ref, gt_ref, dmat_ref, vec_ref, o_ref, sb_ref, st_ref):
    L = RET_CHUNK
    seq = q_ref.shape[1]
    n = seq // L
    hd = pl.program_id(1)
    cf = cdec_ref[hd, 0]
    cb = cdec_ref[hd, 1]
    vec = vec_ref[0]
    qf, kf, qb, kb = vec[:, 0:1], vec[:, 1:2], vec[:, 2:3], vec[:, 3:4]
    contract0 = (((0,), (0,)), ((), ()))

    def scaled(t, col):
        return (t.astype(F32) * col).astype(BF16)

    st_ref[...] = jnp.zeros_like(st_ref)

    def bwd(i, carry):
        c = n - 1 - i
        rs = pl.ds(pl.multiple_of(c * L, L), L)
        sb_ref[c] = st_ref[...].astype(BF16)
        st_ref[...] = cb * st_ref[...] + lax.dot_general(scaled(k_ref[0, rs, :], kb), v_ref[0, rs, :],
                                                         contract0, preferred_element_type=F32)
        return carry

    lax.fori_loop(0, n, bwd, 0, unroll=8)

    st_ref[...] = jnp.zeros_like(st_ref)

    def fwd(c, carry):
        rs = pl.ds(pl.multiple_of(c * L, L), L)
        q_c, k_c, v_c = q_ref[0, rs, :], k_ref[0, rs, :], v_ref[0, rs, :]
        scores = lax.dot_general(q_c, k_c, (((1,), (1,)), ((), ())), preferred_element_type=F32)
        y = jnp.dot((scores * dmat_ref[0]).astype(BF16), v_c, preferred_element_type=F32)
        y = y + jnp.dot(scaled(q_c, qf), st_ref[...].astype(BF16), preferred_element_type=F32)
        y = y + jnp.dot(scaled(q_c, qb), sb_ref[c], preferred_element_type=F32)
        st_ref[...] = cf * st_ref[...] + lax.dot_general(scaled(k_c, kf), v_c, contract0,
                                                         preferred_element_type=F32)
        y = y * lax.rsqrt(jnp.mean(y * y, axis=-1, keepdims=True) + EPS)
        o_ref[0, rs, :] = (gt_ref[0, rs, :].astype(F32) * y).astype(o_ref.dtype)
        return carry

    lax.fori_loop(0, n, fwd, 0, unroll=8)


def _retention(q, k, v, gt, tables):
    dmat, vecs, cdec = tables
    b, s, _ = q.shape
    L = RET_CHUNK
    qk_spec = pl.BlockSpec((1, s, RET_QK), lambda i, h, *_: (i, 0, h))
    v_spec = pl.BlockSpec((1, s, RET_V), lambda i, h, *_: (i, 0, h))
    grid_spec = pltpu.PrefetchScalarGridSpec(
        num_scalar_prefetch=1,
        grid=(b, RET_HEADS),
        in_specs=[qk_spec, qk_spec, v_spec, v_spec,
                  pl.BlockSpec((1, L, L), lambda i, h, *_: (h, 0, 0)),
                  pl.BlockSpec((1, L, 4), lambda i, h, *_: (h, 0, 0))],
        out_specs=v_spec,
        scratch_shapes=[pltpu.VMEM((s // L, RET_QK, RET_V), BF16), pltpu.VMEM((RET_QK, RET_V), F32)],
    )
    return pl.pallas_call(
        _ret_body,
        grid_spec=grid_spec,
        out_shape=jax.ShapeDtypeStruct((b, s, RET_HEADS * RET_V), BF16),
        compiler_params=_cparams(("parallel", "parallel")),
        name="retention",
    )(cdec, q, k, v, gt, dmat, vecs)


def _proj_residual_body(x_ref, y_ref, w_ref, o_ref):
    o_ref[...] = x_ref[...] + jnp.dot(y_ref[...], w_ref[...], preferred_element_type=F32)


def _proj_residual(x, y, w, tm=1024):
    m, d = x.shape
    kd = y.shape[1]
    return pl.pallas_call(
        _proj_residual_body,
        grid=(m // tm,),
        in_specs=[pl.BlockSpec((tm, d), lambda i: (i, 0)), pl.BlockSpec((tm, kd), lambda i: (i, 0)),
                  _const_spec(w.shape)],
        out_specs=pl.BlockSpec((tm, d), lambda i: (i, 0)),
        out_shape=jax.ShapeDtypeStruct((m, d), F32),
        compiler_params=_cparams(("parallel",)),
        name="ret_out_proj",
    )(x, y, w)


def _na_s5_layer(x, bsz, seq, norm, w_in, rpb, lam_re, lam_im, log_dt, b_re, b_im, c_re, c_im, d_skip,
                 w_glu, b_glu, w_out):
    m, d = x.shape
    half = NA_HEADS * NA_HEAD_DIM
    col_scale = jnp.concatenate([jnp.full((half,), NA_HEAD_DIM ** -0.5 * LOG2E, F32),
                                 jnp.ones((3 * half,), F32)])
    z, u_pairs = _ab_in_proj(x.reshape(bsz, seq, d), norm, (w_in * col_scale).astype(BF16), 3 * half)
    a_out = _na_attention(z, _na_bias_table(rpb))
    mats = _s5_matrices(lam_re, lam_im, log_dt, b_re, b_im, c_re, c_im, d_skip)
    pairs, nc = u_pairs.shape[0], u_pairs.shape[1]
    y_pairs = _s5_scan(u_pairs.reshape(pairs, nc * bsz, half), mats, bsz).reshape(pairs, nc, bsz, half)
    out = _ab_out(x.reshape(bsz, seq, d), a_out, y_pairs, w_glu.astype(BF16), b_glu, w_out.astype(BF16))
    return out.reshape(m, d)


def _rope_tables(seq):
    half = RET_QK // 2
    inv_freq = ROPE_BASE ** (-jnp.arange(half, dtype=F32) / half)
    ang = jnp.arange(seq, dtype=F32)[:, None] * inv_freq[None, :]
    return jnp.cos(ang), jnp.sin(ang)


def _retention_layer(x, bsz, seq, norm, w_in, decay, w_out):
    m, d = x.shape
    cos, sin = _rope_tables(seq)
    q, k, v, gt = _ret_in_proj(x, seq, norm, w_in.astype(BF16), cos, sin)
    shp = lambda t: t.reshape(bsz, seq, t.shape[-1])
    y = _retention(shp(q), shp(k), shp(v), shp(gt), _ret_tables(decay))
    return _proj_residual(x, y.reshape(m, -1), w_out.astype(BF16))


def kernel(x, p, ab_norm, ab_w_in, na_rpb, s5_lambda_re, s5_lambda_im, s5_log_dt, s5_b_re, s5_b_im,
           s5_c_re, s5_c_im, s5_d, s5_w_glu, s5_b_glu, ab_w_out, ret_norm, ret_w_in, ret_decay, ret_w_out,
           ffn_norm, ffn_w_up, ffn_conv_w, ffn_conv_b, ffn_w_down, ple_norm, ple_w_gate, ple_w_proj,
           final_norm):
    bsz, seq, d = x.shape
    depth = p.shape[0]
    m = bsz * seq
    xs = x.reshape(m, d)
    p_tok = p.reshape(depth, m, p.shape[-1])
    for i in range(depth):
        j = i // 2
        if i % 2 == 0:
            xs = _na_s5_layer(xs, bsz, seq, ab_norm[j], ab_w_in[j], na_rpb[j], s5_lambda_re[j],
                              s5_lambda_im[j], s5_log_dt[j], s5_b_re[j], s5_b_im[j], s5_c_re[j],
                              s5_c_im[j], s5_d[j], s5_w_glu[j], s5_b_glu[j], ab_w_out[j])
        else:
            xs = _retention_layer(xs, bsz, seq, ret_norm[j], ret_w_in[j], ret_decay[j], ret_w_out[j])
        xs = _ffn_ple(xs, seq, ffn_norm[i], ffn_w_up[i].astype(BF16), ffn_conv_w[i], ffn_conv_b[i],
                      ffn_w_down[i].astype(BF16), p_tok, i, ple_norm[i], ple_w_gate[i].astype(BF16),
                      ple_w_proj[i].astype(BF16), final_norm, final=(i == depth - 1))
    return xs.reshape(bsz, seq, d)
```

```python
import functools

import jax
import jax.numpy as jnp
from jax import lax
from jax.experimental import pallas as pl
from jax.experimental.pallas import tpu as pltpu

F32 = jnp.float32
BF16 = jnp.bfloat16

LANES = 128
EPS = 1e-6
GRID_W = 64
NA_HEADS = 8
NA_HEAD_DIM = 64
NA_WIN_ROWS = 8
NA_WIN_COLS = 16
NA_MASK = -1e30
LOG2E = 1.4426950408889634
NA_HEAD_BLOCK = 4
S5_GROUP = 16
S5_STATE = 64
S5_CHUNK = 16
S5_SLOT = 2 * S5_GROUP
S5_SLOTS = LANES // S5_SLOT
AB_TOKENS = 64
RET_HEADS = 4
RET_QK = 256
RET_V = 512
RET_CHUNK = 256
ROPE_BASE = 10000.0
CONV_HALO = 8
VMEM_LIMIT = 56 * 1024 * 1024


def _cparams(sem):
    return pltpu.CompilerParams(dimension_semantics=sem, vmem_limit_bytes=VMEM_LIMIT)


def _const_spec(shape):
    nd = len(shape)
    return pl.BlockSpec(shape, lambda *_: (0,) * nd, pipeline_mode=pl.Buffered(1))


def _rms(xf, g):
    ms = jnp.mean(xf * xf, axis=-1, keepdims=True)
    return xf * lax.rsqrt(ms + EPS) * g


def _gelu(x):
    return 0.5 * x * (1.0 + lax.erf(x * (2.0 ** -0.5)))


def _sigmoid(x):
    return 0.5 * jnp.tanh(0.5 * x) + 0.5


def _slot_transpose4(a, lane):
    lo = lane < 2 * S5_SLOT
    even = (lane % (2 * S5_SLOT)) < S5_SLOT
    b0 = jnp.where(lo, a[0], pltpu.roll(a[2], 2 * S5_SLOT, 1))
    b2 = jnp.where(lo, pltpu.roll(a[0], 2 * S5_SLOT, 1), a[2])
    b1 = jnp.where(lo, a[1], pltpu.roll(a[3], 2 * S5_SLOT, 1))
    b3 = jnp.where(lo, pltpu.roll(a[1], 2 * S5_SLOT, 1), a[3])
    return [jnp.where(even, b0, pltpu.roll(b1, S5_SLOT, 1)),
            jnp.where(even, pltpu.roll(b0, LANES - S5_SLOT, 1), b1),
            jnp.where(even, b2, pltpu.roll(b3, S5_SLOT, 1)),
            jnp.where(even, pltpu.roll(b2, LANES - S5_SLOT, 1), b3)]


def _ab_in_body(x_ref, g_ref, w_ref, z_ref, u_ref, zs_ref, *, tn):
    bsz, tt, d = x_ref.shape
    h = _rms(x_ref[...].reshape(bsz * tt, d), g_ref[...]).astype(BF16)
    nz = z_ref.shape[2]
    zu = jnp.dot(h, w_ref[:, nz:], preferred_element_type=F32)
    for w in range(zs_ref.shape[0]):
        zs_ref[w] = zu[:, w * LANES:(w + 1) * LANES]
    lane = lax.broadcasted_iota(jnp.int32, (bsz, LANES), 1)
    for cl in range(tt // S5_CHUNK):
        for v in range(S5_CHUNK // S5_SLOTS):
            for w in range(zs_ref.shape[0]):
                pos = cl * S5_CHUNK + S5_SLOTS * v
                outs = _slot_transpose4([zs_ref[w, pl.ds(pos + i, bsz, stride=tt), :]
                                         for i in range(S5_SLOTS)], lane)
                for j in range(S5_SLOTS):
                    u_ref[S5_SLOTS * w + j, cl, :, v * LANES:(v + 1) * LANES] = outs[j].astype(u_ref.dtype)
    for j in range(nz // tn):
        z_ref[:, :, j * tn:(j + 1) * tn] = jnp.dot(
            h, w_ref[:, j * tn:(j + 1) * tn], preferred_element_type=F32
        ).astype(z_ref.dtype).reshape(bsz, tt, tn)


def _ab_in_proj(x, g, w, nz, tn=512):
    bsz, seq, d = x.shape
    nu = w.shape[1] - nz
    pairs = nu // S5_SLOT
    assert S5_CHUNK * S5_SLOT == nu and pairs == (nu // LANES) * S5_SLOTS
    tt = AB_TOKENS
    return pl.pallas_call(
        functools.partial(_ab_in_body, tn=tn),
        grid=(seq // tt,),
        in_specs=[pl.BlockSpec((bsz, tt, d), lambda i: (0, i, 0)), _const_spec((1, d)), _const_spec(w.shape)],
        out_specs=[pl.BlockSpec((bsz, tt, nz), lambda i: (0, i, 0)),
                   pl.BlockSpec((pairs, tt // S5_CHUNK, bsz, nu), lambda i: (0, i, 0, 0))],
        out_shape=[jax.ShapeDtypeStruct((bsz, seq, nz), BF16),
                   jax.ShapeDtypeStruct((pairs, seq // S5_CHUNK, bsz, nu), BF16)],
        scratch_shapes=[pltpu.VMEM((nu // LANES, bsz * tt, LANES), F32)],
        compiler_params=_cparams(("parallel",)),
        name="ab_in_proj",
    )(x, g.reshape(1, d), w)


def _na_bias_table(rpb):
    cols = jnp.arange(GRID_W)
    cs = jnp.clip(cols - NA_WIN_COLS // 2, 0, GRID_W - NA_WIN_COLS)
    j = jnp.arange(GRID_W)
    inwin = (j[None, :] >= cs[:, None]) & (j[None, :] < cs[:, None] + NA_WIN_COLS)
    dc = jnp.clip(j[None, :] - cols[:, None] + NA_WIN_COLS - 1, 0, 2 * NA_WIN_COLS - 2)
    t = (rpb.astype(F32) * LOG2E)[:, :, dc]
    t = jnp.where(inwin[None, None], t, NA_MASK).transpose(0, 2, 1, 3)
    h = rpb.shape[0]
    kw = NA_WIN_ROWS * GRID_W
    return jnp.stack([t[:, :, v:v + NA_WIN_ROWS, :].reshape(h // NA_HEAD_BLOCK, NA_HEAD_BLOCK * GRID_W, kw)
                      for v in range(NA_WIN_ROWS)])


def _na_body(q_ref, k_ref, v_ref, bias_ref, o_ref, *, rows):
    kwin = NA_WIN_ROWS * GRID_W
    bw = NA_HEAD_BLOCK * NA_HEAD_DIM
    head_of_lane = lax.broadcasted_iota(jnp.int32, (GRID_W, bw), 1) // NA_HEAD_DIM

    def row_fn(r, carry):
        rs = jnp.clip(r - NA_WIN_ROWS // 2, 0, rows - NA_WIN_ROWS)
        variant = rs - r + (NA_WIN_ROWS - 1)
        qrow = pl.ds(pl.multiple_of(r * GRID_W, GRID_W), GRID_W)
        krow = pl.ds(pl.multiple_of(rs * GRID_W, GRID_W), kwin)
        for blk in range(NA_HEADS // NA_HEAD_BLOCK):
            sl = slice(blk * bw, (blk + 1) * bw)
            q_p, k_p, v_p = q_ref[0, qrow, sl], k_ref[0, krow, sl], v_ref[0, krow, sl]
            q_all = jnp.concatenate([jnp.where(head_of_lane == hh, q_p, jnp.zeros_like(q_p))
                                     for hh in range(NA_HEAD_BLOCK)], axis=0)
            logits = lax.dot_general(q_all, k_p, (((1,), (1,)), ((), ())), preferred_element_type=F32)
            logits = logits + bias_ref[variant, blk]
            e = jnp.exp2(logits - jnp.max(logits, axis=-1, keepdims=True))
            den = jnp.sum(e, axis=-1, keepdims=True)
            o = jnp.dot(e.astype(BF16), v_p, preferred_element_type=F32) / den
            out = o[0:GRID_W]
            for hh in range(1, NA_HEAD_BLOCK):
                out = jnp.where(head_of_lane == hh, o[hh * GRID_W:(hh + 1) * GRID_W], out)
            o_ref[0, qrow, sl] = out.astype(o_ref.dtype)
        return carry

    lax.fori_loop(0, rows, row_fn, 0, unroll=16)


def _na_attention(z, bias):
    b, s, _ = z.shape
    width = NA_HEADS * NA_HEAD_DIM
    rows = s // GRID_W
    spec = lambda c: pl.BlockSpec((1, s, width), lambda i, c=c: (i, 0, c))
    return pl.pallas_call(
        functools.partial(_na_body, rows=rows),
        grid=(b,),
        in_specs=[spec(0), spec(1), spec(2), _const_spec(bias.shape)],
        out_specs=pl.BlockSpec((1, s, width), lambda i: (i, 0, 0)),
        out_shape=jax.ShapeDtypeStruct((b, s, width), BF16),
        compiler_params=_cparams(("parallel",)),
        name="na_attention",
    )(z, z, z, bias)


def _s5_matrices(lam_re, lam_im, log_dt, b_re, b_im, c_re, c_im, d_skip):
    L, hg, p = S5_CHUNK, S5_GROUP, S5_STATE
    g = lam_re.shape[1]
    gp = g // 2
    tau = jnp.arange(L + 1, dtype=F32)
    eye2 = jnp.eye(2, dtype=F32)
    pw, bbar_t, cc = [], [], []
    for d in range(2):
        lam = lax.complex(lam_re[d].astype(F32), lam_im[d].astype(F32))
        lam_dt = lam * jnp.exp(log_dt[d].astype(F32))[:, None]
        lam_bar = jnp.exp(lam_dt)
        pw.append(jnp.exp(lam_dt[None] * tau[:, None, None]))
        b_c = lax.complex(b_re[d].astype(F32), b_im[d].astype(F32))
        bbar_t.append((((lam_bar - 1.0) / lam)[:, :, None] * b_c).transpose(0, 2, 1))
        cc.append(lax.complex(c_re[d].astype(F32), c_im[d].astype(F32)))

    kf = jnp.real(jnp.einsum('gnp,tgp,ghp->tghn', cc[0], pw[0][:L], bbar_t[0]))
    kb = jnp.real(jnp.einsum('gnp,tgp,ghp->tghn', cc[1], pw[1][:L], bbar_t[1]))
    skip = jnp.eye(hg, dtype=F32)[None] * d_skip.astype(F32).reshape(g, 1, hg)
    tab = jnp.concatenate([kb[1:][::-1], (kf[0] + kb[0] + skip)[None], kf[1:]])
    tab = jnp.einsum('aqihn,ij->qihajn', tab.reshape(2 * L - 1, gp, 2, hg, hg), eye2)
    tab = tab.reshape(gp, 2 * hg, (2 * L - 1) * 2 * hg)
    lag_tab = jnp.pad(tab, ((0, 0), (0, 0), (0, 2 * hg)))

    a_f = pw[0][L - 1 - jnp.arange(L)][:, :, None, :] * bbar_t[0][None]
    a_b = pw[1][jnp.arange(L)][:, :, None, :] * bbar_t[1][None]
    loc = jnp.stack([jnp.real(a_f), jnp.imag(a_f), jnp.real(a_b), jnp.imag(a_b)])

    e_f = cc[0][None] * pw[0][1 + jnp.arange(L)][:, :, None, :]
    e_b = cc[1][None] * pw[1][L - jnp.arange(L)][:, :, None, :]
    cr = jnp.stack([jnp.real(e_f), -jnp.imag(e_f), jnp.real(e_b), -jnp.imag(e_b)])

    lam_l = jnp.stack([jnp.real(pw[0][L]), jnp.imag(pw[0][L]),
                       jnp.real(pw[1][L]), jnp.imag(pw[1][L])])
    lam_l = lam_l.reshape(4, gp, 2 * p).transpose(1, 0, 2)
    return lag_tab, loc.reshape(4, L, gp, 2 * hg, p), cr.reshape(4, L, gp, 2 * hg, p), lam_l


def _s5_body(u_ref, tab_ref, loc_ref, cross_ref, lam_ref, y_ref, mintra_ref, mloc_ref, mcross_ref,
             xloc_ref, xin_ref, *, bsz, tr):
    rows, width = u_ref.shape[1], u_ref.shape[2]
    nc = rows // bsz
    w = lam_ref.shape[2]

    tab = tab_ref[0]
    for s in range(S5_CHUNK):
        off = (S5_CHUNK - 1 - s) * S5_SLOT
        mintra_ref[s * S5_SLOT:(s + 1) * S5_SLOT, :] = tab[:, off:off + width].astype(BF16)
    own = (lax.broadcasted_iota(jnp.int32, (S5_SLOT, w), 0) // S5_GROUP
           == lax.broadcasted_iota(jnp.int32, (S5_SLOT, w), 1) // S5_STATE)
    for part in range(4):
        for s in range(S5_CHUNK):
            dst = (slice(s * S5_SLOT, (s + 1) * S5_SLOT), slice(part * w, (part + 1) * w))
            for tab_ref_, m_ref in ((loc_ref, mloc_ref), (cross_ref, mcross_ref)):
                blk = tab_ref_[part, s]
                m_ref[dst] = jnp.where(own, jnp.concatenate([blk, blk], axis=1), 0.0).astype(BF16)

    def loc_fn(i, carry):
        rs = pl.ds(pl.multiple_of(i * tr, tr), tr)
        xloc = jnp.dot(u_ref[0, rs, :], mloc_ref[...], preferred_element_type=F32)
        for part in range(4):
            xloc_ref[part, rs, :] = xloc[:, part * w:(part + 1) * w]
        return carry

    lax.fori_loop(0, rows // tr, loc_fn, 0)

    afr, afi, abr, abi = lam_ref[0, 0:1, :], lam_ref[0, 1:2, :], lam_ref[0, 2:3, :], lam_ref[0, 3:4, :]

    def carry_fn(i, state):
        sfr, sfi, sbr, sbi = state
        rf = pl.ds(pl.multiple_of(i * bsz, bsz), bsz)
        rb = pl.ds(pl.multiple_of((nc - 1 - i) * bsz, bsz), bsz)
        xin_ref[0, rf, :] = sfr
        xin_ref[1, rf, :] = sfi
        xin_ref[2, rb, :] = sbr
        xin_ref[3, rb, :] = sbi
        nfr = afr * sfr - afi * sfi + xloc_ref[0, rf, :]
        nfi = afr * sfi + afi * sfr + xloc_ref[1, rf, :]
        nbr = abr * sbr - abi * sbi + xloc_ref[2, rb, :]
        nbi = abr * sbi + abi * sbr + xloc_ref[3, rb, :]
        return nfr, nfi, nbr, nbi

    zero = jnp.zeros((bsz, w), F32)
    lax.fori_loop(0, nc, carry_fn, (zero, zero, zero, zero), unroll=2)

    def out_fn(i, carry):
        rs = pl.ds(pl.multiple_of(i * tr, tr), tr)
        y = jnp.dot(u_ref[0, rs, :], mintra_ref[...], preferred_element_type=F32)
        xin = jnp.concatenate([xin_ref[part, rs, :] for part in range(4)], axis=1)
        y = y + lax.dot_general(xin.astype(BF16), mcross_ref[...], (((1,), (1,)), ((), ())),
                                preferred_element_type=F32)
        y_ref[0, rs, :] = y
        return carry

    lax.fori_loop(0, rows // tr, out_fn, 0)


def _s5_scan(u_pairs, mats, bsz):
    lag_tab, loc_tab, cross_tab, lam_l = mats
    gp, rows, width = u_pairs.shape
    tr = min(512, rows)
    w = lam_l.shape[2]
    pair_spec = lambda a: pl.BlockSpec((1,) + a.shape[1:], lambda i: (i, 0, 0))
    state_spec = pl.BlockSpec(loc_tab.shape[:2] + (None,) + loc_tab.shape[3:], lambda i: (0, 0, i, 0, 0))
    return pl.pallas_call(
        functools.partial(_s5_body, bsz=bsz, tr=tr),
        grid=(gp,),
        in_specs=[pair_spec(u_pairs), pair_spec(lag_tab), state_spec, state_spec, pair_spec(lam_l)],
        out_specs=pl.BlockSpec((1, rows, width), lambda i: (i, 0, 0)),
        out_shape=jax.ShapeDtypeStruct((gp, rows, width), F32),
        scratch_shapes=[pltpu.VMEM((width, width), BF16), pltpu.VMEM((width, 4 * w), BF16),
                        pltpu.VMEM((width, 4 * w), BF16),
                        pltpu.VMEM((4, rows, w), F32), pltpu.VMEM((4, rows, w), F32)],
        compiler_params=_cparams(("parallel",)),
        name="s5_scan",
    )(u_pairs, lag_tab, loc_tab, cross_tab, lam_l)


def _ab_out_body(x_ref, a_ref, y_ref, wglu_ref, bglu_ref, wout_ref, o_ref, ys_ref):
    bsz, tt, d = x_ref.shape
    half = a_ref.shape[2]
    a = a_ref[...].reshape(bsz * tt, half)
    acc = x_ref[...].reshape(bsz * tt, d) + jnp.dot(a, wout_ref[0:half, :], preferred_element_type=F32)
    lane = lax.broadcasted_iota(jnp.int32, (bsz, LANES), 1)
    for cl in range(tt // S5_CHUNK):
        for v in range(S5_CHUNK // S5_SLOTS):
            for w in range(half // LANES):
                outs = _slot_transpose4([y_ref[S5_SLOTS * w + j, cl, :, v * LANES:(v + 1) * LANES]
                                         for j in range(S5_SLOTS)], lane)
                pos = cl * S5_CHUNK + S5_SLOTS * v
                for i in range(S5_SLOTS):
                    ys_ref[w, pl.ds(pos + i, bsz, stride=tt), :] = outs[i]
    yg = _gelu(jnp.concatenate([ys_ref[w] for w in range(half // LANES)], axis=1))
    gate = _sigmoid(jnp.dot(yg.astype(BF16), wglu_ref[...], preferred_element_type=F32) + bglu_ref[...])
    b_out = (yg * gate).astype(BF16)
    acc = acc + jnp.dot(b_out, wout_ref[half:, :], preferred_element_type=F32)
    o_ref[...] = acc.reshape(bsz, tt, d)


def _ab_out(x, a, y_pairs, wglu, bglu, wout):
    bsz, seq, d = x.shape
    half = a.shape[2]
    pairs = y_pairs.shape[0]
    tt = AB_TOKENS
    tok = lambda n: pl.BlockSpec((bsz, tt, n), lambda i: (0, i, 0))
    return pl.pallas_call(
        _ab_out_body,
        grid=(seq // tt,),
        in_specs=[tok(d), tok(half),
                  pl.BlockSpec((pairs, tt // S5_CHUNK, bsz, half), lambda i: (0, i, 0, 0)),
                  _const_spec(wglu.shape), _const_spec((1, half)), _const_spec(wout.shape)],
        out_specs=tok(d),
        out_shape=jax.ShapeDtypeStruct((bsz, seq, d), F32),
        scratch_shapes=[pltpu.VMEM((half // LANES, bsz * tt, LANES), F32)],
        compiler_params=_cparams(("parallel",)),
        name="ab_out_proj",
    )(x, a, y_pairs, wglu, bglu.reshape(1, half), wout)


def _ffn_body(x_ref, xp_ref, xn_ref, g_ref, wup_ref, cw_ref, cb_ref, wdn_ref, p_ref, pg_ref, wg_ref,
              wp_ref, fg_ref, o_ref, h_ref, act_ref, *, tiles_per_seq, tf, final):
    tm = x_ref.shape[0]
    dff = wdn_ref.shape[0]
    ext = tm + 2 * CONV_HALO
    i = pl.program_id(0)
    has_prev = (i % tiles_per_seq != 0).astype(F32)
    has_next = (i % tiles_per_seq != tiles_per_seq - 1).astype(F32)
    g = g_ref[...]
    h_ref[0:tm, :] = _rms(x_ref[...], g).astype(BF16)
    halo = jnp.concatenate([_rms(xn_ref[...], g) * has_next, _rms(xp_ref[...], g) * has_prev], axis=0)
    h_ref[tm:ext, :] = halo.astype(BF16)
    h = h_ref[...]

    def conv(u, c0):
        w = cw_ref[:, c0:c0 + tf]
        out = (pltpu.roll(u, 1, 0) * w[0:1] + u * w[1:2] + pltpu.roll(u, ext - 1, 0) * w[2:3]
               + cb_ref[:, c0:c0 + tf])
        return out[0:tm]

    for j in range(dff // tf):
        ua = jnp.dot(h, wup_ref[:, j * tf:(j + 1) * tf], preferred_element_type=F32)
        ug = jnp.dot(h, wup_ref[:, dff + j * tf:dff + (j + 1) * tf], preferred_element_type=F32)
        act_ref[:, j * tf:(j + 1) * tf] = (_gelu(conv(ug, dff + j * tf)) * conv(ua, j * tf)).astype(BF16)
    x = x_ref[...] + jnp.dot(act_ref[...], wdn_ref[...], preferred_element_type=F32)
    emb = jnp.dot(p_ref[...].astype(BF16), wp_ref[...], preferred_element_type=F32)
    gate = _sigmoid(jnp.dot(_rms(x, pg_ref[...]).astype(BF16), wg_ref[...], preferred_element_type=F32))
    out = x + gate * emb
    if final:
        out = _rms(out, fg_ref[...])
    o_ref[...] = out


def _ffn_ple(x, seq, g, wup, cw, cb, wdn, p, layer, pg, wg, wp, fg, final, tm=512, tf=256):
    m, d = x.shape
    dff = wdn.shape[0]
    pd = p.shape[2]
    hb = tm // CONV_HALO
    nblk = m // CONV_HALO
    return pl.pallas_call(
        functools.partial(_ffn_body, tiles_per_seq=seq // tm, tf=tf, final=final),
        grid=(m // tm,),
        in_specs=[pl.BlockSpec((tm, d), lambda i: (i, 0)),
                  pl.BlockSpec((CONV_HALO, d), lambda i: (jnp.maximum(i * hb - 1, 0), 0)),
                  pl.BlockSpec((CONV_HALO, d), lambda i: (jnp.minimum((i + 1) * hb, nblk - 1), 0)),
                  _const_spec((1, d)), _const_spec(wup.shape), _const_spec(cw.shape),
                  _const_spec((1, 2 * dff)), _const_spec(wdn.shape),
                  pl.BlockSpec((None, tm, pd), lambda i: (layer, i, 0)),
                  _const_spec((1, d)), _const_spec(wg.shape), _const_spec(wp.shape), _const_spec((1, d))],
        out_specs=pl.BlockSpec((tm, d), lambda i: (i, 0)),
        out_shape=jax.ShapeDtypeStruct((m, d), F32),
        scratch_shapes=[pltpu.VMEM((tm + 2 * CONV_HALO, d), BF16), pltpu.VMEM((tm, dff), BF16)],
        compiler_params=_cparams(("parallel",)),
        name="ffn_ple",
    )(x, x, x, g.reshape(1, d), wup, cw, cb.reshape(1, 2 * dff), wdn, p, pg.reshape(1, d), wg, wp,
      fg.reshape(1, d))


def _ret_in_body(x_ref, g_ref, w_ref, cos_ref, sin_ref, q_ref, k_ref, v_ref, gt_ref):
    h = _rms(x_ref[...], g_ref[...]).astype(BF16)
    cos, sin = cos_ref[...], sin_ref[...]
    half = RET_QK // 2
    dq = q_ref.shape[1]
    dv = v_ref.shape[1]

    def rot(z, scale):
        x1, x2 = z[:, :half], z[:, half:]
        return jnp.concatenate([(x1 * cos - x2 * sin) * scale, (x1 * sin + x2 * cos) * scale], axis=-1)

    for j in range(dq // RET_QK):
        c0 = j * RET_QK
        zq = jnp.dot(h, w_ref[:, c0:c0 + RET_QK], preferred_element_type=F32)
        q_ref[:, c0:c0 + RET_QK] = rot(zq, 1.0).astype(BF16)
        zk = jnp.dot(h, w_ref[:, dq + c0:dq + c0 + RET_QK], preferred_element_type=F32)
        k_ref[:, c0:c0 + RET_QK] = rot(zk, RET_QK ** -0.5).astype(BF16)
    tn = 512
    for j in range(dv // tn):
        c0 = j * tn
        v_ref[:, c0:c0 + tn] = jnp.dot(h, w_ref[:, 2 * dq + c0:2 * dq + c0 + tn],
                                       preferred_element_type=F32).astype(BF16)
        gate = jnp.dot(h, w_ref[:, 2 * dq + dv + c0:2 * dq + dv + c0 + tn], preferred_element_type=F32)
        gt_ref[:, c0:c0 + tn] = (gate * _sigmoid(gate)).astype(BF16)


def _ret_in_proj(x, seq, g, w, cos, sin, tm=1024):
    m, d = x.shape
    dq = RET_HEADS * RET_QK
    dv = RET_HEADS * RET_V
    tps = seq // tm
    row = lambda n: pl.BlockSpec((tm, n), lambda i: (i, 0))
    pos = pl.BlockSpec((tm, RET_QK // 2), lambda i: (i % tps, 0))
    return pl.pallas_call(
        _ret_in_body,
        grid=(m // tm,),
        in_specs=[row(d), _const_spec((1, d)), _const_spec(w.shape), pos, pos],
        out_specs=[row(dq), row(dq), row(dv), row(dv)],
        out_shape=[jax.ShapeDtypeStruct((m, dq), BF16), jax.ShapeDtypeStruct((m, dq), BF16),
                   jax.ShapeDtypeStruct((m, dv), BF16), jax.ShapeDtypeStruct((m, dv), BF16)],
        compiler_params=_cparams(("parallel",)),
        name="ret_in_proj",
    )(x, g.reshape(1, d), w, cos, sin)


def _ret_tables(decay_param):
    L = RET_CHUNK
    lg = -jnp.exp(decay_param.astype(F32))
    lg_f, lg_b = lg[0][:, None], lg[1][:, None]
    pos = jnp.arange(L, dtype=F32)
    diff = pos[:, None] - pos[None, :]
    dmat = jnp.where(diff >= 0, jnp.exp(lg_f[:, :, None] * jnp.abs(diff)),
                     jnp.exp(lg_b[:, :, None] * jnp.abs(diff)))
    vecs = jnp.stack([jnp.exp(lg_f * (pos + 1.0)),
                      jnp.exp(lg_f * (L - 1.0 - pos)),
                      jnp.exp(lg_b * (L - pos)),
                      jnp.exp(lg_b * pos)], axis=-1)
    cdec = jnp.exp(lg * L).T
    return dmat, vecs, cdec


def _ret_body(cdec_ref, q_ref, k_ref, v_ref, gt_ref, dmat_ref, vec_ref, o_ref, sb_ref, st_ref):
    L = RET_CHUNK
    seq = q_ref.shape[1]
    n = seq // L
    hd = pl.program_id(1)
    cf = cdec_ref[hd, 0]
    cb = cdec_ref[hd, 1]
    vec = vec_ref[0]
    qf, kf, qb, kb = vec[:, 0:1], vec[:, 1:2], vec[:, 2:3], vec[:, 3:4]
    contract0 = (((0,), (0,)), ((), ()))

    def scaled(t, col):
        return (t.astype(F32) * col).astype(BF16)

    st_ref[...] = jnp.zeros_like(st_ref)

    def bwd(i, carry):
        c = n - 1 - i
        rs = pl.ds(pl.multiple_of(c * L, L), L)
        sb_ref[c] = st_ref[...].astype(BF16)
        st_ref[...] = cb * st_ref[...] + lax.dot_general(scaled(k_ref[0, rs, :], kb), v_ref[0, rs, :],
                                                         contract0, preferred_element_type=F32)
        return carry

    lax.fori_loop(0, n, bwd, 0, unroll=8)

    st_ref[...] = jnp.zeros_like(st_ref)

    def fwd(c, carry):
        rs = pl.ds(pl.multiple_of(c * L, L), L)
        q_c, k_c, v_c = q_ref[0, rs, :], k_ref[0, rs, :], v_ref[0, rs, :]
        scores = lax.dot_general(q_c, k_c, (((1,), (1,)), ((), ())), preferred_element_type=F32)
        y = jnp.dot((scores * dmat_ref[0]).astype(BF16), v_c, preferred_element_type=F32)
        y = y + jnp.dot(scaled(q_c, qf), st_ref[...].astype(BF16), preferred_element_type=F32)
        y = y + jnp.dot(scaled(q_c, qb), sb_ref[c], preferred_element_type=F32)
        st_ref[...] = cf * st_ref[...] + lax.dot_general(scaled(k_c, kf), v_c, contract0,
                                                         preferred_element_type=F32)
        y = y * lax.rsqrt(jnp.mean(y * y, axis=-1, keepdims=True) + EPS)
        o_ref[0, rs, :] = (gt_ref[0, rs, :].astype(F32) * y).astype(o_ref.dtype)
        return carry

    lax.fori_loop(0, n, fwd, 0, unroll=8)


def _retention(q, k, v, gt, tables):
    dmat, vecs, cdec = tables
    b, s, _ = q.shape
    L = RET_CHUNK
    qk_spec = pl.BlockSpec((1, s, RET_QK), lambda i, h, *_: (i, 0, h))
    v_spec = pl.BlockSpec((1, s, RET_V), lambda i, h, *_: (i, 0, h))
    grid_spec = pltpu.PrefetchScalarGridSpec(
        num_scalar_prefetch=1,
        grid=(b, RET_HEADS),
        in_specs=[qk_spec, qk_spec, v_spec, v_spec,
                  pl.BlockSpec((1, L, L), lambda i, h, *_: (h, 0, 0)),
                  pl.BlockSpec((1, L, 4), lambda i, h, *_: (h, 0, 0))],
        out_specs=v_spec,
        scratch_shapes=[pltpu.VMEM((s // L, RET_QK, RET_V), BF16), pltpu.VMEM((RET_QK, RET_V), F32)],
    )
    return pl.pallas_call(
        _ret_body,
        grid_spec=grid_spec,
        out_shape=jax.ShapeDtypeStruct((b, s, RET_HEADS * RET_V), BF16),
        compiler_params=_cparams(("parallel", "parallel")),
        name="retention",
    )(cdec, q, k, v, gt, dmat, vecs)


def _proj_residual_body(x_ref, y_ref, w_ref, o_ref):
    o_ref[...] = x_ref[...] + jnp.dot(y_ref[...], w_ref[...], preferred_element_type=F32)


def _proj_residual(x, y, w, tm=1024):
    m, d = x.shape
    kd = y.shape[1]
    return pl.pallas_call(
        _proj_residual_body,
        grid=(m // tm,),
        in_specs=[pl.BlockSpec((tm, d), lambda i: (i, 0)), pl.BlockSpec((tm, kd), lambda i: (i, 0)),
                  _const_spec(w.shape)],
        out_specs=pl.BlockSpec((tm, d), lambda i: (i, 0)),
        out_shape=jax.ShapeDtypeStruct((m, d), F32),
        compiler_params=_cparams(("parallel",)),
        name="ret_out_proj",
    )(x, y, w)


def _na_s5_layer(x, bsz, seq, norm, w_in, rpb, lam_re, lam_im, log_dt, b_re, b_im, c_re, c_im, d_skip,
                 w_glu, b_glu, w_out):
    m, d = x.shape
    half = NA_HEADS * NA_HEAD_DIM
    col_scale = jnp.concatenate([jnp.full((half,), NA_HEAD_DIM ** -0.5 * LOG2E, F32),
                                 jnp.ones((3 * half,), F32)])
    z, u_pairs = _ab_in_proj(x.reshape(bsz, seq, d), norm, (w_in * col_scale).astype(BF16), 3 * half)
    a_out = _na_attention(z, _na_bias_table(rpb))
    mats = _s5_matrices(lam_re, lam_im, log_dt, b_re, b_im, c_re, c_im, d_skip)
    pairs, nc = u_pairs.shape[0], u_pairs.shape[1]
    y_pairs = _s5_scan(u_pairs.reshape(pairs, nc * bsz, half), mats, bsz).reshape(pairs, nc, bsz, half)
    out = _ab_out(x.reshape(bsz, seq, d), a_out, y_pairs, w_glu.astype(BF16), b_glu, w_out.astype(BF16))
    return out.reshape(m, d)


def _rope_tables(seq):
    half = RET_QK // 2
    inv_freq = ROPE_BASE ** (-jnp.arange(half, dtype=F32) / half)
    ang = jnp.arange(seq, dtype=F32)[:, None] * inv_freq[None, :]
    return jnp.cos(ang), jnp.sin(ang)


def _retention_layer(x, bsz, seq, norm, w_in, decay, w_out):
    m, d = x.shape
    cos, sin = _rope_tables(seq)
    q, k, v, gt = _ret_in_proj(x, seq, norm, w_in.astype(BF16), cos, sin)
    shp = lambda t: t.reshape(bsz, seq, t.shape[-1])
    y = _retention(shp(q), shp(k), shp(v), shp(gt), _ret_tables(decay))
    return _proj_residual(x, y.reshape(m, -1), w_out.astype(BF16))


def kernel(x, p, ab_norm, ab_w_in, na_rpb, s5_lambda_re, s5_lambda_im, s5_log_dt, s5_b_re, s5_b_im,
           s5_c_re, s5_c_im, s5_d, s5_w_glu, s5_b_glu, ab_w_out, ret_norm, ret_w_in, ret_decay, ret_w_out,
           ffn_norm, ffn_w_up, ffn_conv_w, ffn_conv_b, ffn_w_down, ple_norm, ple_w_gate, ple_w_proj,
           final_norm):
    bsz, seq, d = x.shape
    depth = p.shape[0]
    m = bsz * seq
    xs = x.reshape(m, d)
    p_tok = p.reshape(depth, m, p.shape[-1])
    for i in range(depth):
        j = i // 2
        if i % 2 == 0:
            xs = _na_s5_layer(xs, bsz, seq, ab_norm[j], ab_w_in[j], na_rpb[j], s5_lambda_re[j],
                              s5_lambda_im[j], s5_log_dt[j], s5_b_re[j], s5_b_im[j], s5_c_re[j],
                              s5_c_im[j], s5_d[j], s5_w_glu[j], s5_b_glu[j], ab_w_out[j])
        else:
            xs = _retention_layer(xs, bsz, seq, ret_norm[j], ret_w_in[j], ret_decay[j], ret_w_out[j])
        xs = _ffn_ple(xs, seq, ffn_norm[i], ffn_w_up[i].astype(BF16), ffn_conv_w[i], ffn_conv_b[i],
                      ffn_w_down[i].astype(BF16), p_tok, i, ple_norm[i], ple_w_gate[i].astype(BF16),
                      ple_w_proj[i].astype(BF16), final_norm, final=(i == depth - 1))
    return xs.reshape(bsz, seq, d)
```

```python
import functools

import jax
import jax.numpy as jnp
from jax import lax
from jax.experimental import pallas as pl
from jax.experimental.pallas import tpu as pltpu

F32 = jnp.float32
BF16 = jnp.bfloat16

LANES = 128
EPS = 1e-6
GRID_W = 64
NA_HEADS = 8
NA_HEAD_DIM = 64
NA_WIN_ROWS = 8
NA_WIN_COLS = 16
NA_MASK = -1e30
LOG2E = 1.4426950408889634
NA_HEAD_BLOCK = 4
S5_GROUP = 16
S5_STATE = 64
S5_CHUNK = 16
S5_SLOT = 2 * S5_GROUP
S5_SLOTS = LANES // S5_SLOT
AB_TOKENS = 64
RET_HEADS = 4
RET_QK = 256
RET_V = 512
RET_CHUNK = 256
ROPE_BASE = 10000.0
CONV_HALO = 8
MXU_DEPTH = 256
FFN_TOKENS = 512
PROJ_TOKENS = 1024
PROJ_COLS = 512
S5_ROWS = 1024
NA_UNROLL = 16
RET_UNROLL = 8
VMEM_LIMIT = 56 * 1024 * 1024


def _cparams(sem):
    return pltpu.CompilerParams(dimension_semantics=sem, vmem_limit_bytes=VMEM_LIMIT)


def _const_spec(shape):
    nd = len(shape)
    return pl.BlockSpec(shape, lambda *_: (0,) * nd, pipeline_mode=pl.Buffered(1))


def _rms(xf, g):
    ms = jnp.mean(xf * xf, axis=-1, keepdims=True)
    return xf * lax.rsqrt(ms + EPS) * g


def _gelu(x):
    return 0.5 * x * (1.0 + lax.erf(x * (2.0 ** -0.5)))


def _sigmoid(x):
    return 0.5 * jnp.tanh(0.5 * x) + 0.5


def _slot_transpose4(a, lane):
    lo = lane < 2 * S5_SLOT
    even = (lane % (2 * S5_SLOT)) < S5_SLOT
    b0 = jnp.where(lo, a[0], pltpu.roll(a[2], 2 * S5_SLOT, 1))
    b2 = jnp.where(lo, pltpu.roll(a[0], 2 * S5_SLOT, 1), a[2])
    b1 = jnp.where(lo, a[1], pltpu.roll(a[3], 2 * S5_SLOT, 1))
    b3 = jnp.where(lo, pltpu.roll(a[1], 2 * S5_SLOT, 1), a[3])
    return [jnp.where(even, b0, pltpu.roll(b1, S5_SLOT, 1)),
            jnp.where(even, pltpu.roll(b0, LANES - S5_SLOT, 1), b1),
            jnp.where(even, b2, pltpu.roll(b3, S5_SLOT, 1)),
            jnp.where(even, pltpu.roll(b2, LANES - S5_SLOT, 1), b3)]


def _ab_in_body(x_ref, g_ref, w_ref, z_ref, u_ref, zs_ref, *, tn):
    bsz, tt, d = x_ref.shape
    h = _rms(x_ref[...].reshape(bsz * tt, d), g_ref[...]).astype(BF16)
    nz = z_ref.shape[2]
    zu = jnp.dot(h, w_ref[:, nz:], preferred_element_type=F32)
    for w in range(zs_ref.shape[0]):
        zs_ref[w] = zu[:, w * LANES:(w + 1) * LANES]
    lane = lax.broadcasted_iota(jnp.int32, (bsz, LANES), 1)
    for cl in range(tt // S5_CHUNK):
        for v in range(S5_CHUNK // S5_SLOTS):
            for w in range(zs_ref.shape[0]):
                pos = cl * S5_CHUNK + S5_SLOTS * v
                outs = _slot_transpose4([zs_ref[w, pl.ds(pos + i, bsz, stride=tt), :]
                                         for i in range(S5_SLOTS)], lane)
                for j in range(S5_SLOTS):
                    u_ref[S5_SLOTS * w + j, cl, :, v * LANES:(v + 1) * LANES] = outs[j].astype(u_ref.dtype)
    for j in range(nz // tn):
        z_ref[:, :, j * tn:(j + 1) * tn] = jnp.dot(
            h, w_ref[:, j * tn:(j + 1) * tn], preferred_element_type=F32
        ).astype(z_ref.dtype).reshape(bsz, tt, tn)


def _ab_in_proj(x, g, w, nz, tn=PROJ_COLS):
    bsz, seq, d = x.shape
    nu = w.shape[1] - nz
    pairs = nu // S5_SLOT
    assert S5_CHUNK * S5_SLOT == nu and pairs == (nu // LANES) * S5_SLOTS
    tt = AB_TOKENS
    return pl.pallas_call(
        functools.partial(_ab_in_body, tn=tn),
        grid=(seq // tt,),
        in_specs=[pl.BlockSpec((bsz, tt, d), lambda i: (0, i, 0)), _const_spec((1, d)), _const_spec(w.shape)],
        out_specs=[pl.BlockSpec((bsz, tt, nz), lambda i: (0, i, 0)),
                   pl.BlockSpec((pairs, tt // S5_CHUNK, bsz, nu), lambda i: (0, i, 0, 0))],
        out_shape=[jax.ShapeDtypeStruct((bsz, seq, nz), BF16),
                   jax.ShapeDtypeStruct((pairs, seq // S5_CHUNK, bsz, nu), BF16)],
        scratch_shapes=[pltpu.VMEM((nu // LANES, bsz * tt, LANES), F32)],
        compiler_params=_cparams(("parallel",)),
        name="ab_in_proj",
    )(x, g.reshape(1, d), w)


def _na_bias_table(rpb):
    cols = jnp.arange(GRID_W)
    cs = jnp.clip(cols - NA_WIN_COLS // 2, 0, GRID_W - NA_WIN_COLS)
    j = jnp.arange(GRID_W)
    inwin = (j[None, :] >= cs[:, None]) & (j[None, :] < cs[:, None] + NA_WIN_COLS)
    dc = jnp.clip(j[None, :] - cols[:, None] + NA_WIN_COLS - 1, 0, 2 * NA_WIN_COLS - 2)
    t = (rpb.astype(F32) * LOG2E)[:, :, dc]
    t = jnp.where(inwin[None, None], t, NA_MASK).transpose(0, 2, 1, 3)
    h = rpb.shape[0]
    kw = NA_WIN_ROWS * GRID_W
    return jnp.stack([t[:, :, v:v + NA_WIN_ROWS, :].reshape(h // NA_HEAD_BLOCK, NA_HEAD_BLOCK * GRID_W, kw)
                      for v in range(NA_WIN_ROWS)])


def _na_body(q_ref, k_ref, v_ref, bias_ref, o_ref, *, rows):
    kwin = NA_WIN_ROWS * GRID_W
    bw = NA_HEAD_BLOCK * NA_HEAD_DIM
    head_of_lane = lax.broadcasted_iota(jnp.int32, (GRID_W, bw), 1) // NA_HEAD_DIM

    def row_fn(r, carry):
        rs = jnp.clip(r - NA_WIN_ROWS // 2, 0, rows - NA_WIN_ROWS)
        variant = rs - r + (NA_WIN_ROWS - 1)
        qrow = pl.ds(pl.multiple_of(r * GRID_W, GRID_W), GRID_W)
        krow = pl.ds(pl.multiple_of(rs * GRID_W, GRID_W), kwin)
        for blk in range(NA_HEADS // NA_HEAD_BLOCK):
            sl = slice(blk * bw, (blk + 1) * bw)
            q_p, k_p, v_p = q_ref[0, qrow, sl], k_ref[0, krow, sl], v_ref[0, krow, sl]
            q_all = jnp.concatenate([jnp.where(head_of_lane == hh, q_p, jnp.zeros_like(q_p))
                                     for hh in range(NA_HEAD_BLOCK)], axis=0)
            logits = lax.dot_general(q_all, k_p, (((1,), (1,)), ((), ())), preferred_element_type=F32)
            logits = logits + bias_ref[variant, blk]
            e = jnp.exp2(logits - jnp.max(logits, axis=-1, keepdims=True))
            den = jnp.sum(e, axis=-1, keepdims=True)
            o = jnp.dot(e.astype(BF16), v_p, preferred_element_type=F32) / den
            out = o[0:GRID_W]
            for hh in range(1, NA_HEAD_BLOCK):
                out = jnp.where(head_of_lane == hh, o[hh * GRID_W:(hh + 1) * GRID_W], out)
            o_ref[0, qrow, sl] = out.astype(o_ref.dtype)
        return carry

    lax.fori_loop(0, rows, row_fn, 0, unroll=NA_UNROLL)


def _na_attention(z, bias):
    b, s, _ = z.shape
    width = NA_HEADS * NA_HEAD_DIM
    rows = s // GRID_W
    spec = lambda c: pl.BlockSpec((1, s, width), lambda i, c=c: (i, 0, c))
    return pl.pallas_call(
        functools.partial(_na_body, rows=rows),
        grid=(b,),
        in_specs=[spec(0), spec(1), spec(2), _const_spec(bias.shape)],
        out_specs=pl.BlockSpec((1, s, width), lambda i: (i, 0, 0)),
        out_shape=jax.ShapeDtypeStruct((b, s, width), BF16),
        compiler_params=_cparams(("parallel",)),
        name="na_attention",
    )(z, z, z, bias)


def _s5_matrices(lam_re, lam_im, log_dt, b_re, b_im, c_re, c_im, d_skip):
    L, hg, p = S5_CHUNK, S5_GROUP, S5_STATE
    g = lam_re.shape[1]
    gp = g // 2
    tau = jnp.arange(L + 1, dtype=F32)
    eye2 = jnp.eye(2, dtype=F32)
    pw, bbar_t, cc = [], [], []
    for d in range(2):
        lam = lax.complex(lam_re[d].astype(F32), lam_im[d].astype(F32))
        lam_dt = lam * jnp.exp(log_dt[d].astype(F32))[:, None]
        lam_bar = jnp.exp(lam_dt)
        pw.append(jnp.exp(lam_dt[None] * tau[:, None, None]))
        b_c = lax.complex(b_re[d].astype(F32), b_im[d].astype(F32))
        bbar_t.append((((lam_bar - 1.0) / lam)[:, :, None] * b_c).transpose(0, 2, 1))
        cc.append(lax.complex(c_re[d].astype(F32), c_im[d].astype(F32)))

    kf = jnp.real(jnp.einsum('gnp,tgp,ghp->tghn', cc[0], pw[0][:L], bbar_t[0]))
    kb = jnp.real(jnp.einsum('gnp,tgp,ghp->tghn', cc[1], pw[1][:L], bbar_t[1]))
    skip = jnp.eye(hg, dtype=F32)[None] * d_skip.astype(F32).reshape(g, 1, hg)
    tab = jnp.concatenate([kb[1:][::-1], (kf[0] + kb[0] + skip)[None], kf[1:]])
    tab = jnp.einsum('aqihn,ij->qihajn', tab.reshape(2 * L - 1, gp, 2, hg, hg), eye2)
    tab = tab.reshape(gp, 2 * hg, (2 * L - 1) * 2 * hg)
    lag_tab = jnp.pad(tab, ((0, 0), (0, 0), (0, 2 * hg)))

    a_f = pw[0][L - 1 - jnp.arange(L)][:, :, None, :] * bbar_t[0][None]
    a_b = pw[1][jnp.arange(L)][:, :, None, :] * bbar_t[1][None]
    loc = jnp.stack([jnp.real(a_f), jnp.imag(a_f), jnp.real(a_b), jnp.imag(a_b)])

    e_f = cc[0][None] * pw[0][1 + jnp.arange(L)][:, :, None, :]
    e_b = cc[1][None] * pw[1][L - jnp.arange(L)][:, :, None, :]
    cr = jnp.stack([jnp.real(e_f), -jnp.imag(e_f), jnp.real(e_b), -jnp.imag(e_b)])

    lam_l = jnp.stack([jnp.real(pw[0][L]), jnp.imag(pw[0][L]),
                       jnp.real(pw[1][L]), jnp.imag(pw[1][L])])
    lam_l = lam_l.reshape(4, gp, 2 * p).transpose(1, 0, 2)
    return lag_tab, loc.reshape(4, L, gp, 2 * hg, p), cr.reshape(4, L, gp, 2 * hg, p), lam_l


def _s5_body(u_ref, tab_ref, loc_ref, cross_ref, lam_ref, y_ref, mintra_ref, mloc_ref, mcross_ref,
             xloc_ref, xin_ref, *, bsz, tr):
    rows, width = u_ref.shape[1], u_ref.shape[2]
    nc = rows // bsz
    w = lam_ref.shape[2]

    tab = tab_ref[0]
    for s in range(S5_CHUNK):
        off = (S5_CHUNK - 1 - s) * S5_SLOT
        mintra_ref[s * S5_SLOT:(s + 1) * S5_SLOT, :] = tab[:, off:off + width].astype(BF16)
    own = (lax.broadcasted_iota(jnp.int32, (S5_SLOT, w), 0) // S5_GROUP
           == lax.broadcasted_iota(jnp.int32, (S5_SLOT, w), 1) // S5_STATE)
    for part in range(4):
        for s in range(S5_CHUNK):
            dst = (slice(s * S5_SLOT, (s + 1) * S5_SLOT), slice(part * w, (part + 1) * w))
            for tab_ref_, m_ref in ((loc_ref, mloc_ref), (cross_ref, mcross_ref)):
                blk = tab_ref_[part, s]
                m_ref[dst] = jnp.where(own, jnp.concatenate([blk, blk], axis=1), 0.0).astype(BF16)

    def loc_fn(i, carry):
        rs = pl.ds(pl.multiple_of(i * tr, tr), tr)
        xloc = jnp.dot(u_ref[0, rs, :], mloc_ref[...], preferred_element_type=F32)
        for part in range(4):
            xloc_ref[part, rs, :] = xloc[:, part * w:(part + 1) * w]
        return carry

    lax.fori_loop(0, rows // tr, loc_fn, 0)

    afr, afi, abr, abi = lam_ref[0, 0:1, :], lam_ref[0, 1:2, :], lam_ref[0, 2:3, :], lam_ref[0, 3:4, :]

    def carry_fn(i, state):
        sfr, sfi, sbr, sbi = state
        rf = pl.ds(pl.multiple_of(i * bsz, bsz), bsz)
        rb = pl.ds(pl.multiple_of((nc - 1 - i) * bsz, bsz), bsz)
        xin_ref[0, rf, :] = sfr
        xin_ref[1, rf, :] = sfi
        xin_ref[2, rb, :] = sbr
        xin_ref[3, rb, :] = sbi
        nfr = afr * sfr - afi * sfi + xloc_ref[0, rf, :]
        nfi = afr * sfi + afi * sfr + xloc_ref[1, rf, :]
        nbr = abr * sbr - abi * sbi + xloc_ref[2, rb, :]
        nbi = abr * sbi + abi * sbr + xloc_ref[3, rb, :]
        return nfr, nfi, nbr, nbi

    zero = jnp.zeros((bsz, w), F32)
    lax.fori_loop(0, nc, carry_fn, (zero, zero, zero, zero), unroll=2)

    def out_fn(i, carry):
        rs = pl.ds(pl.multiple_of(i * tr, tr), tr)
        y = jnp.dot(u_ref[0, rs, :], mintra_ref[...], preferred_element_type=F32)
        xin = jnp.concatenate([xin_ref[part, rs, :] for part in range(4)], axis=1)
        y = y + lax.dot_general(xin.astype(BF16), mcross_ref[...], (((1,), (1,)), ((), ())),
                                preferred_element_type=F32)
        y_ref[0, rs, :] = y
        return carry

    lax.fori_loop(0, rows // tr, out_fn, 0)


def _s5_scan(u_pairs, mats, bsz):
    lag_tab, loc_tab, cross_tab, lam_l = mats
    gp, rows, width = u_pairs.shape
    tr = min(S5_ROWS, rows)
    w = lam_l.shape[2]
    pair_spec = lambda a: pl.BlockSpec((1,) + a.shape[1:], lambda i: (i, 0, 0))
    state_spec = pl.BlockSpec(loc_tab.shape[:2] + (None,) + loc_tab.shape[3:], lambda i: (0, 0, i, 0, 0))
    return pl.pallas_call(
        functools.partial(_s5_body, bsz=bsz, tr=tr),
        grid=(gp,),
        in_specs=[pair_spec(u_pairs), pair_spec(lag_tab), state_spec, state_spec, pair_spec(lam_l)],
        out_specs=pl.BlockSpec((1, rows, width), lambda i: (i, 0, 0)),
        out_shape=jax.ShapeDtypeStruct((gp, rows, width), F32),
        scratch_shapes=[pltpu.VMEM((width, width), BF16), pltpu.VMEM((width, 4 * w), BF16),
                        pltpu.VMEM((width, 4 * w), BF16),
                        pltpu.VMEM((4, rows, w), F32), pltpu.VMEM((4, rows, w), F32)],
        compiler_params=_cparams(("parallel",)),
        name="s5_scan",
    )(u_pairs, lag_tab, loc_tab, cross_tab, lam_l)


def _ab_out_body(x_ref, a_ref, y_ref, wglu_ref, bglu_ref, wout_ref, o_ref, ys_ref):
    bsz, tt, d = x_ref.shape
    half = a_ref.shape[2]
    a = a_ref[...].reshape(bsz * tt, half)
    acc = x_ref[...].reshape(bsz * tt, d) + jnp.dot(a, wout_ref[0:half, :], preferred_element_type=F32)
    lane = lax.broadcasted_iota(jnp.int32, (bsz, LANES), 1)
    for cl in range(tt // S5_CHUNK):
        for v in range(S5_CHUNK // S5_SLOTS):
            for w in range(half // LANES):
                outs = _slot_transpose4([y_ref[S5_SLOTS * w + j, cl, :, v * LANES:(v + 1) * LANES]
                                         for j in range(S5_SLOTS)], lane)
                pos = cl * S5_CHUNK + S5_SLOTS * v
                for i in range(S5_SLOTS):
                    ys_ref[w, pl.ds(pos + i, bsz, stride=tt), :] = outs[i]
    yg = _gelu(jnp.concatenate([ys_ref[w] for w in range(half // LANES)], axis=1))
    gate = _sigmoid(jnp.dot(yg.astype(BF16), wglu_ref[...], preferred_element_type=F32) + bglu_ref[...])
    b_out = (yg * gate).astype(BF16)
    acc = acc + jnp.dot(b_out, wout_ref[half:, :], preferred_element_type=F32)
    o_ref[...] = acc.reshape(bsz, tt, d)


def _ab_out(x, a, y_pairs, wglu, bglu, wout):
    bsz, seq, d = x.shape
    half = a.shape[2]
    pairs = y_pairs.shape[0]
    tt = AB_TOKENS
    tok = lambda n: pl.BlockSpec((bsz, tt, n), lambda i: (0, i, 0))
    return pl.pallas_call(
        _ab_out_body,
        grid=(seq // tt,),
        in_specs=[tok(d), tok(half),
                  pl.BlockSpec((pairs, tt // S5_CHUNK, bsz, half), lambda i: (0, i, 0, 0)),
                  _const_spec(wglu.shape), _const_spec((1, half)), _const_spec(wout.shape)],
        out_specs=tok(d),
        out_shape=jax.ShapeDtypeStruct((bsz, seq, d), F32),
        scratch_shapes=[pltpu.VMEM((half // LANES, bsz * tt, LANES), F32)],
        compiler_params=_cparams(("parallel",)),
        name="ab_out_proj",
    )(x, a, y_pairs, wglu, bglu.reshape(1, half), wout)


def _ffn_body(x_ref, xp_ref, xn_ref, g_ref, wup_ref, cw_ref, cb_ref, wdn_ref, p_ref, pg_ref, wg_ref,
              wp_ref, fg_ref, o_ref, h_ref, act_ref, *, tiles_per_seq, tf, final):
    tm = x_ref.shape[0]
    dff = wdn_ref.shape[0]
    ext = tm + 2 * CONV_HALO
    i = pl.program_id(0)
    has_prev = (i % tiles_per_seq != 0).astype(F32)
    has_next = (i % tiles_per_seq != tiles_per_seq - 1).astype(F32)
    g = g_ref[...]
    h_ref[0:tm, :] = _rms(x_ref[...], g).astype(BF16)
    halo = jnp.concatenate([_rms(xn_ref[...], g) * has_next, _rms(xp_ref[...], g) * has_prev], axis=0)
    h_ref[tm:ext, :] = halo.astype(BF16)
    h = h_ref[...]

    def conv(u, c0):
        w = cw_ref[:, c0:c0 + tf]
        out = (pltpu.roll(u, 1, 0) * w[0:1] + u * w[1:2] + pltpu.roll(u, ext - 1, 0) * w[2:3]
               + cb_ref[:, c0:c0 + tf])
        return out[0:tm]

    for j in range(dff // tf):
        ua = jnp.dot(h, wup_ref[:, j * tf:(j + 1) * tf], preferred_element_type=F32)
        ug = jnp.dot(h, wup_ref[:, dff + j * tf:dff + (j + 1) * tf], preferred_element_type=F32)
        act_ref[:, j * tf:(j + 1) * tf] = (_gelu(conv(ug, dff + j * tf)) * conv(ua, j * tf)).astype(BF16)
    x = x_ref[...] + jnp.dot(act_ref[...], wdn_ref[...], preferred_element_type=F32)
    emb = jnp.dot(p_ref[...].astype(BF16), wp_ref[...], preferred_element_type=F32)
    gate = _sigmoid(jnp.dot(_rms(x, pg_ref[...]).astype(BF16), wg_ref[...], preferred_element_type=F32))
    out = x + gate * emb
    if final:
        out = _rms(out, fg_ref[...])
    o_ref[...] = out


def _ffn_ple(x, seq, g, wup, cw, cb, wdn, p, layer, pg, wg, wp, fg, final, tm=FFN_TOKENS, tf=MXU_DEPTH):
    m, d = x.shape
    dff = wdn.shape[0]
    pd = p.shape[2]
    hb = tm // CONV_HALO
    nblk = m // CONV_HALO
    return pl.pallas_call(
        functools.partial(_ffn_body, tiles_per_seq=seq // tm, tf=tf, final=final),
        grid=(m // tm,),
        in_specs=[pl.BlockSpec((tm, d), lambda i: (i, 0)),
                  pl.BlockSpec((CONV_HALO, d), lambda i: (jnp.maximum(i * hb - 1, 0), 0)),
                  pl.BlockSpec((CONV_HALO, d), lambda i: (jnp.minimum((i + 1) * hb, nblk - 1), 0)),
                  _const_spec((1, d)), _const_spec(wup.shape), _const_spec(cw.shape),
                  _const_spec((1, 2 * dff)), _const_spec(wdn.shape),
                  pl.BlockSpec((None, tm, pd), lambda i: (layer, i, 0)),
                  _const_spec((1, d)), _const_spec(wg.shape), _const_spec(wp.shape), _const_spec((1, d))],
        out_specs=pl.BlockSpec((tm, d), lambda i: (i, 0)),
        out_shape=jax.ShapeDtypeStruct((m, d), F32),
        scratch_shapes=[pltpu.VMEM((tm + 2 * CONV_HALO, d), BF16), pltpu.VMEM((tm, dff), BF16)],
        compiler_params=_cparams(("parallel",)),
        name="ffn_ple",
    )(x, x, x, g.reshape(1, d), wup, cw, cb.reshape(1, 2 * dff), wdn, p, pg.reshape(1, d), wg, wp,
      fg.reshape(1, d))


def _ret_in_body(x_ref, g_ref, w_ref, cos_ref, sin_ref, q_ref, k_ref, v_ref, gt_ref):
    h = _rms(x_ref[...], g_ref[...]).astype(BF16)
    cos, sin = cos_ref[...], sin_ref[...]
    half = RET_QK // 2
    dq = q_ref.shape[1]
    dv = v_ref.shape[1]

    def rot(z, scale):
        x1, x2 = z[:, :half], z[:, half:]
        return jnp.concatenate([(x1 * cos - x2 * sin) * scale, (x1 * sin + x2 * cos) * scale], axis=-1)

    for j in range(dq // RET_QK):
        c0 = j * RET_QK
        zq = jnp.dot(h, w_ref[:, c0:c0 + RET_QK], preferred_element_type=F32)
        q_ref[:, c0:c0 + RET_QK] = rot(zq, 1.0).astype(BF16)
        zk = jnp.dot(h, w_ref[:, dq + c0:dq + c0 + RET_QK], preferred_element_type=F32)
        k_ref[:, c0:c0 + RET_QK] = rot(zk, RET_QK ** -0.5).astype(BF16)
    tn = PROJ_COLS
    for j in range(dv // tn):
        c0 = j * tn
        v_ref[:, c0:c0 + tn] = jnp.dot(h, w_ref[:, 2 * dq + c0:2 * dq + c0 + tn],
                                       preferred_element_type=F32).astype(BF16)
        gate = jnp.dot(h, w_ref[:, 2 * dq + dv + c0:2 * dq + dv + c0 + tn], preferred_element_type=F32)
        gt_ref[:, c0:c0 + tn] = (gate * _sigmoid(gate)).astype(BF16)


def _ret_in_proj(x, seq, g, w, cos, sin, tm=PROJ_TOKENS):
    m, d = x.shape
    dq = RET_HEADS * RET_QK
    dv = RET_HEADS * RET_V
    tps = seq // tm
    row = lambda n: pl.BlockSpec((tm, n), lambda i: (i, 0))
    pos = pl.BlockSpec((tm, RET_QK // 2), lambda i: (i % tps, 0))
    return pl.pallas_call(
        _ret_in_body,
        grid=(m // tm,),
        in_specs=[row(d), _const_spec((1, d)), _const_spec(w.shape), pos, pos],
        out_specs=[row(dq), row(dq), row(dv), row(dv)],
        out_shape=[jax.ShapeDtypeStruct((m, dq), BF16), jax.ShapeDtypeStruct((m, dq), BF16),
                   jax.ShapeDtypeStruct((m, dv), BF16), jax.ShapeDtypeStruct((m, dv), BF16)],
        compiler_params=_cparams(("parallel",)),
        name="ret_in_proj",
    )(x, g.reshape(1, d), w, cos, sin)


def _ret_tables(decay_param):
    L = RET_CHUNK
    lg = -jnp.exp(decay_param.astype(F32))
    lg_f, lg_b = lg[0][:, None], lg[1][:, None]
    pos = jnp.arange(L, dtype=F32)
    diff = pos[:, None] - pos[None, :]
    dmat = jnp.where(diff >= 0, jnp.exp(lg_f[:, :, None] * jnp.abs(diff)),
                     jnp.exp(lg_b[:, :, None] * jnp.abs(diff)))
    vecs = jnp.stack([jnp.exp(lg_f * (pos + 1.0)),
                      jnp.exp(lg_f * (L - 1.0 - pos)),
                      jnp.exp(lg_b * (L - pos)),
                      jnp.exp(lg_b * pos)], axis=-1)
    cdec = jnp.exp(lg * L).T
    return dmat, vecs, cdec


def _ret_body(cdec_ref, q_ref, k_ref, v_ref, gt_ref, dmat_ref, vec_ref, o_ref, sb_ref, st_ref):
    L = RET_CHUNK
    seq = q_ref.shape[1]
    n = seq // L
    hd = pl.program_id(1)
    cf = cdec_ref[hd, 0]
    cb = cdec_ref[hd, 1]
    vec = vec_ref[0]
    qf, kf, qb, kb = vec[:, 0:1], vec[:, 1:2], vec[:, 2:3], vec[:, 3:4]
    contract0 = (((0,), (0,)), ((), ()))

    def scaled(t, col):
        return (t.astype(F32) * col).astype(BF16)

    st_ref[...] = jnp.zeros_like(st_ref)

    def bwd(i, carry):
        c = n - 1 - i
        rs = pl.ds(pl.multiple_of(c * L, L), L)
        sb_ref[c] = st_ref[...].astype(BF16)
        st_ref[...] = cb * st_ref[...] + lax.dot_general(scaled(k_ref[0, rs, :], kb), v_ref[0, rs, :],
                                                         contract0, preferred_element_type=F32)
        return carry

    lax.fori_loop(0, n, bwd, 0, unroll=RET_UNROLL)

    st_ref[...] = jnp.zeros_like(st_ref)

    def fwd(c, carry):
        rs = pl.ds(pl.multiple_of(c * L, L), L)
        q_c, k_c, v_c = q_ref[0, rs, :], k_ref[0, rs, :], v_ref[0, rs, :]
        scores = lax.dot_general(q_c, k_c, (((1,), (1,)), ((), ())), preferred_element_type=F32)
        y = jnp.dot((scores * dmat_ref[0]).astype(BF16), v_c, preferred_element_type=F32)
        y = y + jnp.dot(scaled(q_c, qf), st_ref[...].astype(BF16), preferred_element_type=F32)
        y = y + jnp.dot(scaled(q_c, qb), sb_ref[c], preferred_element_type=F32)
        st_ref[...] = cf * st_ref[...] + lax.dot_general(scaled(k_c, kf), v_c, contract0,
                                                         preferred_element_type=F32)
        y = y * lax.rsqrt(jnp.mean(y * y, axis=-1, keepdims=True) + EPS)
        o_ref[0, rs, :] = (gt_ref[0, rs, :].astype(F32) * y).astype(o_ref.dtype)
        return carry

    lax.fori_loop(0, n, fwd, 0, unroll=RET_UNROLL)


def _retention(q, k, v, gt, tables):
    dmat, vecs, cdec = tables
    b, s, _ = q.shape
    L = RET_CHUNK
    qk_spec = pl.BlockSpec((1, s, RET_QK), lambda i, h, *_: (i, 0, h))
    v_spec = pl.BlockSpec((1, s, RET_V), lambda i, h, *_: (i, 0, h))
    grid_spec = pltpu.PrefetchScalarGridSpec(
        num_scalar_prefetch=1,
        grid=(b, RET_HEADS),
        in_specs=[qk_spec, qk_spec, v_spec, v_spec,
                  pl.BlockSpec((1, L, L), lambda i, h, *_: (h, 0, 0)),
                  pl.BlockSpec((1, L, 4), lambda i, h, *_: (h, 0, 0))],
        out_specs=v_spec,
        scratch_shapes=[pltpu.VMEM((s // L, RET_QK, RET_V), BF16), pltpu.VMEM((RET_QK, RET_V), F32)],
    )
    return pl.pallas_call(
        _ret_body,
        grid_spec=grid_spec,
        out_shape=jax.ShapeDtypeStruct((b, s, RET_HEADS * RET_V), BF16),
        compiler_params=_cparams(("parallel", "parallel")),
        name="retention",
    )(cdec, q, k, v, gt, dmat, vecs)


def _proj_residual_body(x_ref, y_ref, w_ref, o_ref):
    o_ref[...] = x_ref[...] + jnp.dot(y_ref[...], w_ref[...], preferred_element_type=F32)


def _proj_residual(x, y, w, tm=PROJ_TOKENS):
    m, d = x.shape
    kd = y.shape[1]
    return pl.pallas_call(
        _proj_residual_body,
        grid=(m // tm,),
        in_specs=[pl.BlockSpec((tm, d), lambda i: (i, 0)), pl.BlockSpec((tm, kd), lambda i: (i, 0)),
                  _const_spec(w.shape)],
        out_specs=pl.BlockSpec((tm, d), lambda i: (i, 0)),
        out_shape=jax.ShapeDtypeStruct((m, d), F32),
        compiler_params=_cparams(("parallel",)),
        name="ret_out_proj",
    )(x, y, w)


def _na_s5_layer(x, bsz, seq, norm, w_in, rpb, lam_re, lam_im, log_dt, b_re, b_im, c_re, c_im, d_skip,
                 w_glu, b_glu, w_out):
    m, d = x.shape
    half = NA_HEADS * NA_HEAD_DIM
    col_scale = jnp.concatenate([jnp.full((half,), NA_HEAD_DIM ** -0.5 * LOG2E, F32),
                                 jnp.ones((3 * half,), F32)])
    z, u_pairs = _ab_in_proj(x.reshape(bsz, seq, d), norm, (w_in * col_scale).astype(BF16), 3 * half)
    a_out = _na_attention(z, _na_bias_table(rpb))
    mats = _s5_matrices(lam_re, lam_im, log_dt, b_re, b_im, c_re, c_im, d_skip)
    pairs, nc = u_pairs.shape[0], u_pairs.shape[1]
    y_pairs = _s5_scan(u_pairs.reshape(pairs, nc * bsz, half), mats, bsz).reshape(pairs, nc, bsz, half)
    out = _ab_out(x.reshape(bsz, seq, d), a_out, y_pairs, w_glu.astype(BF16), b_glu, w_out.astype(BF16))
    return out.reshape(m, d)


def _rope_tables(seq):
    half = RET_QK // 2
    inv_freq = ROPE_BASE ** (-jnp.arange(half, dtype=F32) / half)
    ang = jnp.arange(seq, dtype=F32)[:, None] * inv_freq[None, :]
    return jnp.cos(ang), jnp.sin(ang)


def _retention_layer(x, bsz, seq, norm, w_in, decay, w_out):
    m, d = x.shape
    cos, sin = _rope_tables(seq)
    q, k, v, gt = _ret_in_proj(x, seq, norm, w_in.astype(BF16), cos, sin)
    shp = lambda t: t.reshape(bsz, seq, t.shape[-1])
    y = _retention(shp(q), shp(k), shp(v), shp(gt), _ret_tables(decay))
    return _proj_residual(x, y.reshape(m, -1), w_out.astype(BF16))


def kernel(x, p, ab_norm, ab_w_in, na_rpb, s5_lambda_re, s5_lambda_im, s5_log_dt, s5_b_re, s5_b_im,
           s5_c_re, s5_c_im, s5_d, s5_w_glu, s5_b_glu, ab_w_out, ret_norm, ret_w_in, ret_decay, ret_w_out,
           ffn_norm, ffn_w_up, ffn_conv_w, ffn_conv_b, ffn_w_down, ple_norm, ple_w_gate, ple_w_proj,
           final_norm):
    bsz, seq, d = x.shape
    depth = p.shape[0]
    m = bsz * seq
    xs = x.reshape(m, d)
    p_tok = p.reshape(depth, m, p.shape[-1])
    for i in range(depth):
        j = i // 2
        if i % 2 == 0:
            xs = _na_s5_layer(xs, bsz, seq, ab_norm[j], ab_w_in[j], na_rpb[j], s5_lambda_re[j],
                              s5_lambda_im[j], s5_log_dt[j], s5_b_re[j], s5_b_im[j], s5_c_re[j],
                              s5_c_im[j], s5_d[j], s5_w_glu[j], s5_b_glu[j], ab_w_out[j])
        else:
            xs = _retention_layer(xs, bsz, seq, ret_norm[j], ret_w_in[j], ret_decay[j], ret_w_out[j])
        xs = _ffn_ple(xs, seq, ffn_norm[i], ffn_w_up[i].astype(BF16), ffn_conv_w[i], ffn_conv_b[i],
                      ffn_w_down[i].astype(BF16), p_tok, i, ple_norm[i], ple_w_gate[i].astype(BF16),
                      ple_w_proj[i].astype(BF16), final_norm, final=(i == depth - 1))
    return xs.reshape(bsz, seq, d)
```

```python
import functools

import jax
import jax.numpy as jnp
from jax import lax
from jax.experimental import pallas as pl
from jax.experimental.pallas import tpu as pltpu

F32 = jnp.float32
BF16 = jnp.bfloat16

LANES = 128
EPS = 1e-6
GRID_W = 64
NA_HEADS = 8
NA_HEAD_DIM = 64
NA_WIN_ROWS = 8
NA_WIN_COLS = 16
NA_MASK = -1e30
LOG2E = 1.4426950408889634
NA_HEAD_BLOCK = 4
S5_GROUP = 16
S5_STATE = 64
S5_CHUNK = 16
S5_SLOT = 2 * S5_GROUP
S5_SLOTS = LANES // S5_SLOT
AB_TOKENS = 64
RET_HEADS = 4
RET_QK = 256
RET_V = 512
RET_CHUNK = 256
ROPE_BASE = 10000.0
CONV_HALO = 8
MXU_DEPTH = 256
FFN_TOKENS = 512
PROJ_TOKENS = 1024
PROJ_COLS = 512
S5_ROWS = 2048
NA_UNROLL = 16
RET_UNROLL = 8
VMEM_LIMIT = 56 * 1024 * 1024


def _cparams(sem):
    return pltpu.CompilerParams(dimension_semantics=sem, vmem_limit_bytes=VMEM_LIMIT)


def _const_spec(shape):
    nd = len(shape)
    return pl.BlockSpec(shape, lambda *_: (0,) * nd, pipeline_mode=pl.Buffered(1))


def _rms(xf, g):
    ms = jnp.mean(xf * xf, axis=-1, keepdims=True)
    return xf * lax.rsqrt(ms + EPS) * g


def _gelu(x):
    half = 0.5 * x
    return half + half * lax.erf(x * (2.0 ** -0.5))


def _sigmoid(x):
    return 0.5 * jnp.tanh(0.5 * x) + 0.5


def _slot_transpose4(a, lane):
    lo = lane < 2 * S5_SLOT
    even = (lane % (2 * S5_SLOT)) < S5_SLOT
    b0 = jnp.where(lo, a[0], pltpu.roll(a[2], 2 * S5_SLOT, 1))
    b2 = jnp.where(lo, pltpu.roll(a[0], 2 * S5_SLOT, 1), a[2])
    b1 = jnp.where(lo, a[1], pltpu.roll(a[3], 2 * S5_SLOT, 1))
    b3 = jnp.where(lo, pltpu.roll(a[1], 2 * S5_SLOT, 1), a[3])
    return [jnp.where(even, b0, pltpu.roll(b1, S5_SLOT, 1)),
            jnp.where(even, pltpu.roll(b0, LANES - S5_SLOT, 1), b1),
            jnp.where(even, b2, pltpu.roll(b3, S5_SLOT, 1)),
            jnp.where(even, pltpu.roll(b2, LANES - S5_SLOT, 1), b3)]


def _ab_in_body(x_ref, g_ref, w_ref, z_ref, u_ref, zs_ref, *, tn):
    bsz, tt, d = x_ref.shape
    h = _rms(x_ref[...].reshape(bsz * tt, d), g_ref[...]).astype(BF16)
    nz = z_ref.shape[2]
    zu = jnp.dot(h, w_ref[:, nz:], preferred_element_type=F32)
    for w in range(zs_ref.shape[0]):
        zs_ref[w] = zu[:, w * LANES:(w + 1) * LANES]
    lane = lax.broadcasted_iota(jnp.int32, (bsz, LANES), 1)
    for cl in range(tt // S5_CHUNK):
        for v in range(S5_CHUNK // S5_SLOTS):
            for w in range(zs_ref.shape[0]):
                pos = cl * S5_CHUNK + S5_SLOTS * v
                outs = _slot_transpose4([zs_ref[w, pl.ds(pos + i, bsz, stride=tt), :]
                                         for i in range(S5_SLOTS)], lane)
                for j in range(S5_SLOTS):
                    u_ref[S5_SLOTS * w + j, cl, :, v * LANES:(v + 1) * LANES] = outs[j].astype(u_ref.dtype)
    for j in range(nz // tn):
        z_ref[:, :, j * tn:(j + 1) * tn] = jnp.dot(
            h, w_ref[:, j * tn:(j + 1) * tn], preferred_element_type=F32
        ).astype(z_ref.dtype).reshape(bsz, tt, tn)


def _ab_in_proj(x, g, w, nz, tn=PROJ_COLS):
    bsz, seq, d = x.shape
    nu = w.shape[1] - nz
    pairs = nu // S5_SLOT
    assert S5_CHUNK * S5_SLOT == nu and pairs == (nu // LANES) * S5_SLOTS
    tt = AB_TOKENS
    return pl.pallas_call(
        functools.partial(_ab_in_body, tn=tn),
        grid=(seq // tt,),
        in_specs=[pl.BlockSpec((bsz, tt, d), lambda i: (0, i, 0)), _const_spec((1, d)), _const_spec(w.shape)],
        out_specs=[pl.BlockSpec((bsz, tt, nz), lambda i: (0, i, 0)),
                   pl.BlockSpec((pairs, tt // S5_CHUNK, bsz, nu), lambda i: (0, i, 0, 0))],
        out_shape=[jax.ShapeDtypeStruct((bsz, seq, nz), BF16),
                   jax.ShapeDtypeStruct((pairs, seq // S5_CHUNK, bsz, nu), BF16)],
        scratch_shapes=[pltpu.VMEM((nu // LANES, bsz * tt, LANES), F32)],
        compiler_params=_cparams(("parallel",)),
        name="ab_in_proj",
    )(x, g.reshape(1, d), w)


def _na_bias_table(rpb):
    cols = jnp.arange(GRID_W)
    cs = jnp.clip(cols - NA_WIN_COLS // 2, 0, GRID_W - NA_WIN_COLS)
    j = jnp.arange(GRID_W)
    inwin = (j[None, :] >= cs[:, None]) & (j[None, :] < cs[:, None] + NA_WIN_COLS)
    dc = jnp.clip(j[None, :] - cols[:, None] + NA_WIN_COLS - 1, 0, 2 * NA_WIN_COLS - 2)
    onehot = (dc[None] == jnp.arange(2 * NA_WIN_COLS - 1)[:, None, None]).astype(F32)
    t = jnp.einsum('hdk,kcj->hdcj', rpb.astype(F32) * LOG2E, onehot,
                   precision=lax.Precision.HIGHEST)
    t = jnp.where(inwin[None, None], t, NA_MASK).transpose(0, 2, 1, 3)
    h = rpb.shape[0]
    kw = NA_WIN_ROWS * GRID_W
    return jnp.stack([t[:, :, v:v + NA_WIN_ROWS, :].reshape(h // NA_HEAD_BLOCK, NA_HEAD_BLOCK * GRID_W, kw)
                      for v in range(NA_WIN_ROWS)])


def _na_body(q_ref, k_ref, v_ref, bias_ref, o_ref, *, rows):
    kwin = NA_WIN_ROWS * GRID_W
    bw = NA_HEAD_BLOCK * NA_HEAD_DIM
    head_of_lane = lax.broadcasted_iota(jnp.int32, (GRID_W, bw), 1) // NA_HEAD_DIM

    def row_fn(r, carry):
        rs = jnp.clip(r - NA_WIN_ROWS // 2, 0, rows - NA_WIN_ROWS)
        variant = rs - r + (NA_WIN_ROWS - 1)
        qrow = pl.ds(pl.multiple_of(r * GRID_W, GRID_W), GRID_W)
        krow = pl.ds(pl.multiple_of(rs * GRID_W, GRID_W), kwin)
        for blk in range(NA_HEADS // NA_HEAD_BLOCK):
            sl = slice(blk * bw, (blk + 1) * bw)
            q_p, k_p, v_p = q_ref[0, qrow, sl], k_ref[0, krow, sl], v_ref[0, krow, sl]
            q_all = jnp.concatenate([jnp.where(head_of_lane == hh, q_p, jnp.zeros_like(q_p))
                                     for hh in range(NA_HEAD_BLOCK)], axis=0)
            logits = lax.dot_general(q_all, k_p, (((1,), (1,)), ((), ())), preferred_element_type=F32)
            logits = logits + bias_ref[variant, blk]
            e = jnp.exp2(logits - jnp.max(logits, axis=-1, keepdims=True))
            den = jnp.sum(e, axis=-1, keepdims=True)
            o = jnp.dot(e.astype(BF16), v_p, preferred_element_type=F32) / den
            out = o[0:GRID_W]
            for hh in range(1, NA_HEAD_BLOCK):
                out = jnp.where(head_of_lane == hh, o[hh * GRID_W:(hh + 1) * GRID_W], out)
            o_ref[0, qrow, sl] = out.astype(o_ref.dtype)
        return carry

    lax.fori_loop(0, rows, row_fn, 0, unroll=NA_UNROLL)


def _na_attention(z, bias):
    b, s, _ = z.shape
    width = NA_HEADS * NA_HEAD_DIM
    rows = s // GRID_W
    spec = lambda c: pl.BlockSpec((1, s, width), lambda i, c=c: (i, 0, c))
    return pl.pallas_call(
        functools.partial(_na_body, rows=rows),
        grid=(b,),
        in_specs=[spec(0), spec(1), spec(2), _const_spec(bias.shape)],
        out_specs=pl.BlockSpec((1, s, width), lambda i: (i, 0, 0)),
        out_shape=jax.ShapeDtypeStruct((b, s, width), BF16),
        compiler_params=_cparams(("parallel",)),
        name="na_attention",
    )(z, z, z, bias)


def _s5_matrices(lam_re, lam_im, log_dt, b_re, b_im, c_re, c_im, d_skip):
    L, hg, p = S5_CHUNK, S5_GROUP, S5_STATE
    g = lam_re.shape[1]
    gp = g // 2
    tau = jnp.arange(L + 1, dtype=F32)
    eye2 = jnp.eye(2, dtype=F32)
    pw, bbar_t, cc = [], [], []
    for d in range(2):
        lam = lax.complex(lam_re[d].astype(F32), lam_im[d].astype(F32))
        lam_dt = lam * jnp.exp(log_dt[d].astype(F32))[:, None]
        lam_bar = jnp.exp(lam_dt)
        pw.append(jnp.exp(lam_dt[None] * tau[:, None, None]))
        b_c = lax.complex(b_re[d].astype(F32), b_im[d].astype(F32))
        bbar_t.append((((lam_bar - 1.0) / lam)[:, :, None] * b_c).transpose(0, 2, 1))
        cc.append(lax.complex(c_re[d].astype(F32), c_im[d].astype(F32)))

    kf = jnp.real(jnp.einsum('gnp,tgp,ghp->tghn', cc[0], pw[0][:L], bbar_t[0]))
    kb = jnp.real(jnp.einsum('gnp,tgp,ghp->tghn', cc[1], pw[1][:L], bbar_t[1]))
    skip = jnp.eye(hg, dtype=F32)[None] * d_skip.astype(F32).reshape(g, 1, hg)
    tab = jnp.concatenate([kb[1:][::-1], (kf[0] + kb[0] + skip)[None], kf[1:]])
    tab = jnp.einsum('aqihn,ij->qihajn', tab.reshape(2 * L - 1, gp, 2, hg, hg), eye2)
    tab = tab.reshape(gp, 2 * hg, (2 * L - 1) * 2 * hg)
    lag_tab = jnp.pad(tab, ((0, 0), (0, 0), (0, 2 * hg)))

    a_f = pw[0][L - 1 - jnp.arange(L)][:, :, None, :] * bbar_t[0][None]
    a_b = pw[1][jnp.arange(L)][:, :, None, :] * bbar_t[1][None]
    loc = jnp.stack([jnp.real(a_f), jnp.imag(a_f), jnp.real(a_b), jnp.imag(a_b)])

    e_f = cc[0][None] * pw[0][1 + jnp.arange(L)][:, :, None, :]
    e_b = cc[1][None] * pw[1][L - jnp.arange(L)][:, :, None, :]
    cr = jnp.stack([jnp.real(e_f), -jnp.imag(e_f), jnp.real(e_b), -jnp.imag(e_b)])

    lam_l = jnp.stack([jnp.real(pw[0][L]), jnp.imag(pw[0][L]),
                       jnp.real(pw[1][L]), jnp.imag(pw[1][L])])
    lam_l = lam_l.reshape(4, gp, 2 * p).transpose(1, 0, 2)
    return lag_tab, loc.reshape(4, L, gp, 2 * hg, p), cr.reshape(4, L, gp, 2 * hg, p), lam_l


def _s5_body(u_ref, tab_ref, loc_ref, cross_ref, lam_ref, y_ref, mintra_ref, mloc_ref, mcross_ref,
             xloc_ref, xin_ref, *, bsz, tr):
    rows, width = u_ref.shape[1], u_ref.shape[2]
    nc = rows // bsz
    w = lam_ref.shape[2]

    tab = tab_ref[0]
    for s in range(S5_CHUNK):
        off = (S5_CHUNK - 1 - s) * S5_SLOT
        mintra_ref[s * S5_SLOT:(s + 1) * S5_SLOT, :] = tab[:, off:off + width].astype(BF16)
    own = (lax.broadcasted_iota(jnp.int32, (S5_SLOT, w), 0) // S5_GROUP
           == lax.broadcasted_iota(jnp.int32, (S5_SLOT, w), 1) // S5_STATE)
    for part in range(4):
        for s in range(S5_CHUNK):
            dst = (slice(s * S5_SLOT, (s + 1) * S5_SLOT), slice(part * w, (part + 1) * w))
            for tab_ref_, m_ref in ((loc_ref, mloc_ref), (cross_ref, mcross_ref)):
                blk = tab_ref_[part, s]
                m_ref[dst] = jnp.where(own, jnp.concatenate([blk, blk], axis=1), 0.0).astype(BF16)

    def loc_fn(i, carry):
        rs = pl.ds(pl.multiple_of(i * tr, tr), tr)
        xloc = jnp.dot(u_ref[0, rs, :], mloc_ref[...], preferred_element_type=F32)
        for part in range(4):
            xloc_ref[part, rs, :] = xloc[:, part * w:(part + 1) * w]
        return carry

    lax.fori_loop(0, rows // tr, loc_fn, 0)

    afr, afi, abr, abi = lam_ref[0, 0:1, :], lam_ref[0, 1:2, :], lam_ref[0, 2:3, :], lam_ref[0, 3:4, :]

    def carry_fn(i, state):
        sfr, sfi, sbr, sbi = state
        rf = pl.ds(pl.multiple_of(i * bsz, bsz), bsz)
        rb = pl.ds(pl.multiple_of((nc - 1 - i) * bsz, bsz), bsz)
        xin_ref[0, rf, :] = sfr
        xin_ref[1, rf, :] = sfi
        xin_ref[2, rb, :] = sbr
        xin_ref[3, rb, :] = sbi
        nfr = afr * sfr - afi * sfi + xloc_ref[0, rf, :]
        nfi = afr * sfi + afi * sfr + xloc_ref[1, rf, :]
        nbr = abr * sbr - abi * sbi + xloc_ref[2, rb, :]
        nbi = abr * sbi + abi * sbr + xloc_ref[3, rb, :]
        return nfr, nfi, nbr, nbi

    zero = jnp.zeros((bsz, w), F32)
    lax.fori_loop(0, nc, carry_fn, (zero, zero, zero, zero), unroll=2)

    def out_fn(i, carry):
        rs = pl.ds(pl.multiple_of(i * tr, tr), tr)
        y = jnp.dot(u_ref[0, rs, :], mintra_ref[...], preferred_element_type=F32)
        xin = jnp.concatenate([xin_ref[part, rs, :] for part in range(4)], axis=1)
        y = y + lax.dot_general(xin.astype(BF16), mcross_ref[...], (((1,), (1,)), ((), ())),
                                preferred_element_type=F32)
        y_ref[0, rs, :] = y
        return carry

    lax.fori_loop(0, rows // tr, out_fn, 0)


def _s5_scan(u_pairs, mats, bsz):
    lag_tab, loc_tab, cross_tab, lam_l = mats
    gp, rows, width = u_pairs.shape
    tr = min(S5_ROWS, rows)
    w = lam_l.shape[2]
    pair_spec = lambda a: pl.BlockSpec((1,) + a.shape[1:], lambda i: (i, 0, 0))
    state_spec = pl.BlockSpec(loc_tab.shape[:2] + (None,) + loc_tab.shape[3:], lambda i: (0, 0, i, 0, 0))
    return pl.pallas_call(
        functools.partial(_s5_body, bsz=bsz, tr=tr),
        grid=(gp,),
        in_specs=[pair_spec(u_pairs), pair_spec(lag_tab), state_spec, state_spec, pair_spec(lam_l)],
        out_specs=pl.BlockSpec((1, rows, width), lambda i: (i, 0, 0)),
        out_shape=jax.ShapeDtypeStruct((gp, rows, width), F32),
        scratch_shapes=[pltpu.VMEM((width, width), BF16), pltpu.VMEM((width, 4 * w), BF16),
                        pltpu.VMEM((width, 4 * w), BF16),
                        pltpu.VMEM((4, rows, w), F32), pltpu.VMEM((4, rows, w), F32)],
        compiler_params=_cparams(("parallel",)),
        name="s5_scan",
    )(u_pairs, lag_tab, loc_tab, cross_tab, lam_l)


def _ab_out_body(x_ref, a_ref, y_ref, wglu_ref, bglu_ref, wout_ref, o_ref, ys_ref):
    bsz, tt, d = x_ref.shape
    half = a_ref.shape[2]
    a = a_ref[...].reshape(bsz * tt, half)
    acc = x_ref[...].reshape(bsz * tt, d) + jnp.dot(a, wout_ref[0:half, :], preferred_element_type=F32)
    lane = lax.broadcasted_iota(jnp.int32, (bsz, LANES), 1)
    for cl in range(tt // S5_CHUNK):
        for v in range(S5_CHUNK // S5_SLOTS):
            for w in range(half // LANES):
                outs = _slot_transpose4([y_ref[S5_SLOTS * w + j, cl, :, v * LANES:(v + 1) * LANES]
                                         for j in range(S5_SLOTS)], lane)
                pos = cl * S5_CHUNK + S5_SLOTS * v
                for i in range(S5_SLOTS):
                    ys_ref[w, pl.ds(pos + i, bsz, stride=tt), :] = outs[i]
    yg = _gelu(jnp.concatenate([ys_ref[w] for w in range(half // LANES)], axis=1))
    gate = _sigmoid(jnp.dot(yg.astype(BF16), wglu_ref[...], preferred_element_type=F32) + bglu_ref[...])
    b_out = (yg * gate).astype(BF16)
    acc = acc + jnp.dot(b_out, wout_ref[half:, :], preferred_element_type=F32)
    o_ref[...] = acc.reshape(bsz, tt, d)


def _ab_out(x, a, y_pairs, wglu, bglu, wout):
    bsz, seq, d = x.shape
    half = a.shape[2]
    pairs = y_pairs.shape[0]
    tt = AB_TOKENS
    tok = lambda n: pl.BlockSpec((bsz, tt, n), lambda i: (0, i, 0))
    return pl.pallas_call(
        _ab_out_body,
        grid=(seq // tt,),
        in_specs=[tok(d), tok(half),
                  pl.BlockSpec((pairs, tt // S5_CHUNK, bsz, half), lambda i: (0, i, 0, 0)),
                  _const_spec(wglu.shape), _const_spec((1, half)), _const_spec(wout.shape)],
        out_specs=tok(d),
        out_shape=jax.ShapeDtypeStruct((bsz, seq, d), F32),
        scratch_shapes=[pltpu.VMEM((half // LANES, bsz * tt, LANES), F32)],
        compiler_params=_cparams(("parallel",)),
        name="ab_out_proj",
    )(x, a, y_pairs, wglu, bglu.reshape(1, half), wout)


def _ffn_body(x_ref, xp_ref, xn_ref, g_ref, wup_ref, cw_ref, cb_ref, wdn_ref, p_ref, pg_ref, wg_ref,
              wp_ref, fg_ref, o_ref, h_ref, act_ref, *, tiles_per_seq, tf, final):
    tm = x_ref.shape[0]
    dff = wdn_ref.shape[0]
    ext = tm + 2 * CONV_HALO
    i = pl.program_id(0)
    has_prev = (i % tiles_per_seq != 0).astype(F32)
    has_next = (i % tiles_per_seq != tiles_per_seq - 1).astype(F32)
    g = g_ref[...]
    h_ref[0:tm, :] = _rms(x_ref[...], g).astype(BF16)
    halo = jnp.concatenate([_rms(xn_ref[...], g) * has_next, _rms(xp_ref[...], g) * has_prev], axis=0)
    h_ref[tm:ext, :] = halo.astype(BF16)
    h = h_ref[...]

    def conv(u, c0):
        w = cw_ref[:, c0:c0 + tf]
        out = (pltpu.roll(u, 1, 0) * w[0:1] + u * w[1:2] + pltpu.roll(u, ext - 1, 0) * w[2:3]
               + cb_ref[:, c0:c0 + tf])
        return out[0:tm]

    for j in range(dff // tf):
        ua = jnp.dot(h, wup_ref[:, j * tf:(j + 1) * tf], preferred_element_type=F32)
        ug = jnp.dot(h, wup_ref[:, dff + j * tf:dff + (j + 1) * tf], preferred_element_type=F32)
        act_ref[:, j * tf:(j + 1) * tf] = (_gelu(conv(ug, dff + j * tf)) * conv(ua, j * tf)).astype(BF16)
    x = x_ref[...] + jnp.dot(act_ref[...], wdn_ref[...], preferred_element_type=F32)
    emb = jnp.dot(p_ref[...].astype(BF16), wp_ref[...], preferred_element_type=F32)
    gate = _sigmoid(jnp.dot(_rms(x, pg_ref[...]).astype(BF16), wg_ref[...], preferred_element_type=F32))
    out = x + gate * emb
    if final:
        out = _rms(out, fg_ref[...])
    o_ref[...] = out


def _ffn_ple(x, seq, g, wup, cw, cb, wdn, p, layer, pg, wg, wp, fg, final, tm=FFN_TOKENS, tf=MXU_DEPTH):
    m, d = x.shape
    dff = wdn.shape[0]
    pd = p.shape[2]
    hb = tm // CONV_HALO
    nblk = m // CONV_HALO
    return pl.pallas_call(
        functools.partial(_ffn_body, tiles_per_seq=seq // tm, tf=tf, final=final),
        grid=(m // tm,),
        in_specs=[pl.BlockSpec((tm, d), lambda i: (i, 0)),
                  pl.BlockSpec((CONV_HALO, d), lambda i: (jnp.maximum(i * hb - 1, 0), 0)),
                  pl.BlockSpec((CONV_HALO, d), lambda i: (jnp.minimum((i + 1) * hb, nblk - 1), 0)),
                  _const_spec((1, d)), _const_spec(wup.shape), _const_spec(cw.shape),
                  _const_spec((1, 2 * dff)), _const_spec(wdn.shape),
                  pl.BlockSpec((None, tm, pd), lambda i: (layer, i, 0)),
                  _const_spec((1, d)), _const_spec(wg.shape), _const_spec(wp.shape), _const_spec((1, d))],
        out_specs=pl.BlockSpec((tm, d), lambda i: (i, 0)),
        out_shape=jax.ShapeDtypeStruct((m, d), F32),
        scratch_shapes=[pltpu.VMEM((tm + 2 * CONV_HALO, d), BF16), pltpu.VMEM((tm, dff), BF16)],
        compiler_params=_cparams(("parallel",)),
        name="ffn_ple",
    )(x, x, x, g.reshape(1, d), wup, cw, cb.reshape(1, 2 * dff), wdn, p, pg.reshape(1, d), wg, wp,
      fg.reshape(1, d))


def _ret_in_body(x_ref, g_ref, w_ref, cos_ref, sin_ref, q_ref, k_ref, v_ref, gt_ref):
    h = _rms(x_ref[...], g_ref[...]).astype(BF16)
    cos, sin = cos_ref[...], sin_ref[...]
    half = RET_QK // 2
    dq = q_ref.shape[1]
    dv = v_ref.shape[1]

    def rot(z, scale):
        x1, x2 = z[:, :half], z[:, half:]
        return jnp.concatenate([(x1 * cos - x2 * sin) * scale, (x1 * sin + x2 * cos) * scale], axis=-1)

    for j in range(dq // RET_QK):
        c0 = j * RET_QK
        zq = jnp.dot(h, w_ref[:, c0:c0 + RET_QK], preferred_element_type=F32)
        q_ref[:, c0:c0 + RET_QK] = rot(zq, 1.0).astype(BF16)
        zk = jnp.dot(h, w_ref[:, dq + c0:dq + c0 + RET_QK], preferred_element_type=F32)
        k_ref[:, c0:c0 + RET_QK] = rot(zk, RET_QK ** -0.5).astype(BF16)
    tn = PROJ_COLS
    for j in range(dv // tn):
        c0 = j * tn
        v_ref[:, c0:c0 + tn] = jnp.dot(h, w_ref[:, 2 * dq + c0:2 * dq + c0 + tn],
                                       preferred_element_type=F32).astype(BF16)
        gate = jnp.dot(h, w_ref[:, 2 * dq + dv + c0:2 * dq + dv + c0 + tn], preferred_element_type=F32)
        gt_ref[:, c0:c0 + tn] = (gate * _sigmoid(gate)).astype(BF16)


def _ret_in_proj(x, seq, g, w, cos, sin, tm=PROJ_TOKENS):
    m, d = x.shape
    dq = RET_HEADS * RET_QK
    dv = RET_HEADS * RET_V
    tps = seq // tm
    row = lambda n: pl.BlockSpec((tm, n), lambda i: (i, 0))
    pos = pl.BlockSpec((tm, RET_QK // 2), lambda i: (i % tps, 0))
    return pl.pallas_call(
        _ret_in_body,
        grid=(m // tm,),
        in_specs=[row(d), _const_spec((1, d)), _const_spec(w.shape), pos, pos],
        out_specs=[row(dq), row(dq), row(dv), row(dv)],
        out_shape=[jax.ShapeDtypeStruct((m, dq), BF16), jax.ShapeDtypeStruct((m, dq), BF16),
                   jax.ShapeDtypeStruct((m, dv), BF16), jax.ShapeDtypeStruct((m, dv), BF16)],
        compiler_params=_cparams(("parallel",)),
        name="ret_in_proj",
    )(x, g.reshape(1, d), w, cos, sin)


def _ret_tables(decay_param):
    L = RET_CHUNK
    lg = -jnp.exp(decay_param.astype(F32))
    lg_f, lg_b = lg[0][:, None], lg[1][:, None]
    pos = jnp.arange(L, dtype=F32)
    diff = pos[:, None] - pos[None, :]
    dmat = jnp.where(diff >= 0, jnp.exp(lg_f[:, :, None] * jnp.abs(diff)),
                     jnp.exp(lg_b[:, :, None] * jnp.abs(diff)))
    vecs = jnp.stack([jnp.exp(lg_f * (pos + 1.0)),
                      jnp.exp(lg_f * (L - 1.0 - pos)),
                      jnp.exp(lg_b * (L - pos)),
                      jnp.exp(lg_b * pos)], axis=-1)
    cdec = jnp.exp(lg * L).T
    return dmat, vecs, cdec


def _ret_body(cdec_ref, q_ref, k_ref, v_ref, gt_ref, dmat_ref, vec_ref, o_ref, sb_ref, st_ref):
    L = RET_CHUNK
    seq = q_ref.shape[1]
    n = seq // L
    hd = pl.program_id(1)
    cf = cdec_ref[hd, 0]
    cb = cdec_ref[hd, 1]
    vec = vec_ref[0]
    qf, kf, qb, kb = vec[:, 0:1], vec[:, 1:2], vec[:, 2:3], vec[:, 3:4]
    contract0 = (((0,), (0,)), ((), ()))

    def scaled(t, col):
        return (t.astype(F32) * col).astype(BF16)

    st_ref[...] = jnp.zeros_like(st_ref)

    def bwd(i, carry):
        c = n - 1 - i
        rs = pl.ds(pl.multiple_of(c * L, L), L)
        sb_ref[c] = st_ref[...].astype(BF16)
        st_ref[...] = cb * st_ref[...] + lax.dot_general(scaled(k_ref[0, rs, :], kb), v_ref[0, rs, :],
                                                         contract0, preferred_element_type=F32)
        return carry

    lax.fori_loop(0, n, bwd, 0, unroll=RET_UNROLL)

    st_ref[...] = jnp.zeros_like(st_ref)

    def fwd(c, carry):
        rs = pl.ds(pl.multiple_of(c * L, L), L)
        q_c, k_c, v_c = q_ref[0, rs, :], k_ref[0, rs, :], v_ref[0, rs, :]
        scores = lax.dot_general(q_c, k_c, (((1,), (1,)), ((), ())), preferred_element_type=F32)
        y = jnp.dot((scores * dmat_ref[0]).astype(BF16), v_c, preferred_element_type=F32)
        y = y + jnp.dot(scaled(q_c, qf), st_ref[...].astype(BF16), preferred_element_type=F32)
        y = y + jnp.dot(scaled(q_c, qb), sb_ref[c], preferred_element_type=F32)
        st_ref[...] = cf * st_ref[...] + lax.dot_general(scaled(k_c, kf), v_c, contract0,
                                                         preferred_element_type=F32)
        y = y * lax.rsqrt(jnp.mean(y * y, axis=-1, keepdims=True) + EPS)
        o_ref[0, rs, :] = (gt_ref[0, rs, :].astype(F32) * y).astype(o_ref.dtype)
        return carry

    lax.fori_loop(0, n, fwd, 0, unroll=RET_UNROLL)


def _retention(q, k, v, gt, tables):
    dmat, vecs, cdec = tables
    b, s, _ = q.shape
    L = RET_CHUNK
    qk_spec = pl.BlockSpec((1, s, RET_QK), lambda i, h, *_: (i, 0, h))
    v_spec = pl.BlockSpec((1, s, RET_V), lambda i, h, *_: (i, 0, h))
    grid_spec = pltpu.PrefetchScalarGridSpec(
        num_scalar_prefetch=1,
        grid=(b, RET_HEADS),
        in_specs=[qk_spec, qk_spec, v_spec, v_spec,
                  pl.BlockSpec((1, L, L), lambda i, h, *_: (h, 0, 0)),
                  pl.BlockSpec((1, L, 4), lambda i, h, *_: (h, 0, 0))],
        out_specs=v_spec,
        scratch_shapes=[pltpu.VMEM((s // L, RET_QK, RET_V), BF16), pltpu.VMEM((RET_QK, RET_V), F32)],
    )
    return pl.pallas_call(
        _ret_body,
        grid_spec=grid_spec,
        out_shape=jax.ShapeDtypeStruct((b, s, RET_HEADS * RET_V), BF16),
        compiler_params=_cparams(("parallel", "parallel")),
        name="retention",
    )(cdec, q, k, v, gt, dmat, vecs)


def _proj_residual_body(x_ref, y_ref, w_ref, o_ref):
    o_ref[...] = x_ref[...] + jnp.dot(y_ref[...], w_ref[...], preferred_element_type=F32)


def _proj_residual(x, y, w, tm=PROJ_TOKENS):
    m, d = x.shape
    kd = y.shape[1]
    return pl.pallas_call(
        _proj_residual_body,
        grid=(m // tm,),
        in_specs=[pl.BlockSpec((tm, d), lambda i: (i, 0)), pl.BlockSpec((tm, kd), lambda i: (i, 0)),
                  _const_spec(w.shape)],
        out_specs=pl.BlockSpec((tm, d), lambda i: (i, 0)),
        out_shape=jax.ShapeDtypeStruct((m, d), F32),
        compiler_params=_cparams(("parallel",)),
        name="ret_out_proj",
    )(x, y, w)


def _na_s5_layer(x, bsz, seq, norm, w_in, rpb, lam_re, lam_im, log_dt, b_re, b_im, c_re, c_im, d_skip,
                 w_glu, b_glu, w_out):
    m, d = x.shape
    half = NA_HEADS * NA_HEAD_DIM
    col_scale = jnp.concatenate([jnp.full((half,), NA_HEAD_DIM ** -0.5 * LOG2E, F32),
                                 jnp.ones((3 * half,), F32)])
    z, u_pairs = _ab_in_proj(x.reshape(bsz, seq, d), norm, (w_in * col_scale).astype(BF16), 3 * half)
    a_out = _na_attention(z, _na_bias_table(rpb))
    mats = _s5_matrices(lam_re, lam_im, log_dt, b_re, b_im, c_re, c_im, d_skip)
    pairs, nc = u_pairs.shape[0], u_pairs.shape[1]
    y_pairs = _s5_scan(u_pairs.reshape(pairs, nc * bsz, half), mats, bsz).reshape(pairs, nc, bsz, half)
    out = _ab_out(x.reshape(bsz, seq, d), a_out, y_pairs, w_glu.astype(BF16), b_glu, w_out.astype(BF16))
    return out.reshape(m, d)


def _rope_tables(seq):
    half = RET_QK // 2
    inv_freq = ROPE_BASE ** (-jnp.arange(half, dtype=F32) / half)
    ang = jnp.arange(seq, dtype=F32)[:, None] * inv_freq[None, :]
    return jnp.cos(ang), jnp.sin(ang)


def _retention_layer(x, bsz, seq, norm, w_in, decay, w_out):
    m, d = x.shape
    cos, sin = _rope_tables(seq)
    q, k, v, gt = _ret_in_proj(x, seq, norm, w_in.astype(BF16), cos, sin)
    shp = lambda t: t.reshape(bsz, seq, t.shape[-1])
    y = _retention(shp(q), shp(k), shp(v), shp(gt), _ret_tables(decay))
    return _proj_residual(x, y.reshape(m, -1), w_out.astype(BF16))


def kernel(x, p, ab_norm, ab_w_in, na_rpb, s5_lambda_re, s5_lambda_im, s5_log_dt, s5_b_re, s5_b_im,
           s5_c_re, s5_c_im, s5_d, s5_w_glu, s5_b_glu, ab_w_out, ret_norm, ret_w_in, ret_decay, ret_w_out,
           ffn_norm, ffn_w_up, ffn_conv_w, ffn_conv_b, ffn_w_down, ple_norm, ple_w_gate, ple_w_proj,
           final_norm):
    bsz, seq, d = x.shape
    depth = p.shape[0]
    m = bsz * seq
    xs = x.reshape(m, d)
    p_tok = p.reshape(depth, m, p.shape[-1])
    for i in range(depth):
        j = i // 2
        if i % 2 == 0:
            xs = _na_s5_layer(xs, bsz, seq, ab_norm[j], ab_w_in[j], na_rpb[j], s5_lambda_re[j],
                              s5_lambda_im[j], s5_log_dt[j], s5_b_re[j], s5_b_im[j], s5_c_re[j],
                              s5_c_im[j], s5_d[j], s5_w_glu[j], s5_b_glu[j], ab_w_out[j])
        else:
            xs = _retention_layer(xs, bsz, seq, ret_norm[j], ret_w_in[j], ret_decay[j], ret_w_out[j])
        xs = _ffn_ple(xs, seq, ffn_norm[i], ffn_w_up[i].astype(BF16), ffn_conv_w[i], ffn_conv_b[i],
                      ffn_w_down[i].astype(BF16), p_tok, i, ple_norm[i], ple_w_gate[i].astype(BF16),
                      ple_w_proj[i].astype(BF16), final_norm, final=(i == depth - 1))
    return xs.reshape(bsz, seq, d)
```
